```python
import math
import jax, jax.numpy as jnp
from jax import lax
import numpy as np

D_MODEL = 2048
BATCH = 2
SEQ = 4096
DEPTH = 4
DEC_BATCH = 8
DEC_SEQ = 8
PAST_LEN = 16384
PAGE_SIZE = 128

N_MIXERS = 4
N_A = (DEPTH + 3) // 4
N_B = (DEPTH + 2) // 4
N_C = (DEPTH + 1) // 4
N_D = DEPTH // 4
CONV_W = 3
N_HEADS = 16
HEAD_DIM = D_MODEL // N_HEADS
DIFF_HEADS = N_HEADS // 2
DIFF_DIM = D_MODEL // (2 * DIFF_HEADS)
IDX_HEADS = 16
IDX_DIM = 64
TOPK_MAX = 256
D_FF = 5632
ROPE_THETA = 10000.0
NORM_EPS = 1e-6
QBLOCK = 128
D_IN_DSA = 3 * D_MODEL + IDX_HEADS * IDX_DIM + IDX_DIM + IDX_HEADS

kernel_name = 'hybrid_conv_sb_diff_dsa_decoder_step'


def rmsnorm(x, g):
    xf = x.astype(jnp.float32)
    y = xf * lax.rsqrt(jnp.mean(xf * xf, axis=-1, keepdims=True) + NORM_EPS)
    return (y * g.astype(jnp.float32)).astype(x.dtype)


def rope(x, pos):
    half = x.shape[-1] // 2
    inv = ROPE_THETA ** (-jnp.arange(half, dtype=jnp.float32) / half)
    ang = pos.astype(jnp.float32)[:, None] * inv
    shape = (pos.shape[0],) + (1,) * (x.ndim - 3) + (half,)
    cos = jnp.cos(ang).reshape(shape)
    sin = jnp.sin(ang).reshape(shape)
    xf = x.astype(jnp.float32)
    x1, x2 = xf[..., :half], xf[..., half:]
    return jnp.concatenate([x1 * cos - x2 * sin, x2 * cos + x1 * sin], axis=-1).astype(x.dtype)


def causal_dwconv(x_ext, w):
    t = x_ext.shape[1] - (CONV_W - 1)
    y = w[0] * x_ext[:, 0:t]
    for i in range(1, CONV_W):
        y = y + w[i] * x_ext[:, i:i + t]
    return y


def with_history(hist, u):
    ext = jnp.concatenate([hist.astype(u.dtype), u], axis=1)
    return ext, ext[:, -(CONV_W - 1):]


def short_conv_mixer(h, hist, w_in, conv_w, w_out):
    b_gate, c_gate, hh = jnp.split(h @ w_in, 3, axis=-1)
    ext, new_hist = with_history(hist, c_gate * hh)
    return (b_gate * causal_dwconv(ext, conv_w)) @ w_out, new_hist


def gated_conv_ffn(h, hist, w_up, conv_w, w_down):
    ext, new_hist = with_history(hist, h @ w_up)
    g, u = jnp.split(causal_dwconv(ext, conv_w), 2, axis=-1)
    return (jax.nn.silu(g) * u) @ w_down, new_hist


def split_heads_qkv(z):
    q, k, v = jnp.split(z, 3, axis=-1)
    b, t = z.shape[0], z.shape[1]
    return (q.reshape(b, t, N_HEADS, HEAD_DIM), k.reshape(b, t, N_HEADS, HEAD_DIM),
            v.reshape(b, t, N_HEADS, HEAD_DIM))


def merge_heads(o):
    return o.reshape(o.shape[0], o.shape[1], D_MODEL)


def take_rows(arr, idx):
    return jax.vmap(lambda a, i: a[i])(arr, idx)


def gather_pages(pool, layer, page_table):
    g = pool[layer, page_table]
    return g.reshape((page_table.shape[0], page_table.shape[1] * PAGE_SIZE) + pool.shape[3:])


def gather_selected(pool, layer, page_table, new_rows, idx):
    is_past = idx < PAST_LEN
    pidx = jnp.minimum(idx, PAST_LEN - 1)
    phys_page = take_rows(page_table, pidx // PAGE_SIZE)
    past = pool[layer, phys_page, pidx % PAGE_SIZE]
    new = take_rows(new_rows, jnp.clip(idx - PAST_LEN, 0, new_rows.shape[1] - 1))
    return jnp.where(is_past[..., None, None], past, new)


def map_query_blocks(fn, qs, n_q):
    nb = n_q // QBLOCK
    blocks = tuple(jnp.moveaxis(a.reshape((a.shape[0], nb, QBLOCK) + a.shape[2:]), 1, 0) for a in qs)
    starts = jnp.arange(nb, dtype=jnp.int32) * QBLOCK
    out = lax.map(lambda args: fn(*args[0], args[1]), (blocks, starts))
    out = jnp.moveaxis(out, 0, 1)
    return out.reshape((out.shape[0], n_q) + out.shape[3:])


def stick_breaking_attend(q, k, v, valid):
    z = jnp.einsum('bthd,bshd->bhts', q, k).astype(jnp.float32) * (HEAD_DIM ** -0.5)
    log_1m = jnp.where(valid, jax.nn.log_sigmoid(-z), 0.0)
    after = lax.cumsum(log_1m, axis=3, reverse=True) - log_1m
    a = jnp.where(valid, jnp.exp(jax.nn.log_sigmoid(z) + after), 0.0)
    return jnp.einsum('bhts,bshd->bthd', a.astype(v.dtype), v)


def diff_project(h, w_qkv, pos):
    q, k, v = jnp.split(h @ w_qkv, 3, axis=-1)
    b, t = h.shape[0], h.shape[1]
    q = rope(q.reshape(b, t, DIFF_HEADS, 2, DIFF_DIM), pos)
    k = rope(k.reshape(b, t, DIFF_HEADS, 2, DIFF_DIM), pos)
    return q, k, v.reshape(b, t, DIFF_HEADS, 2 * DIFF_DIM)


def diff_attend(q, k, v, valid, lam):
    s = jnp.einsum('btnid,bsnid->bnits', q, k).astype(jnp.float32) * (DIFF_DIM ** -0.5)
    p = jax.nn.softmax(jnp.where(valid, s, -jnp.inf), axis=-1)
    a = p[:, :, 0] - lam * p[:, :, 1]
    return jnp.einsum('bnts,bsnd->btnd', a.astype(v.dtype), v)


def dsa_project(h, w_in, pos):
    b, t = h.shape[0], h.shape[1]
    z = h @ w_in
    o1 = 3 * D_MODEL
    o2 = o1 + IDX_HEADS * IDX_DIM
    o3 = o2 + IDX_DIM
    q, k, v = split_heads_qkv(z[..., :o1])
    iq = z[..., o1:o2].reshape(b, t, IDX_HEADS, IDX_DIM)
    ik = z[..., o2:o3]
    iw = z[..., o3:]
    return rope(q, pos), rope(k, pos), v, rope(iq, pos), rope(ik, pos), iw


def indexer_scores(iq, iw, ik):
    qk = jax.nn.relu(jnp.einsum('bthd,bsd->bths', iq, ik).astype(jnp.float32) * (IDX_DIM ** -0.5))
    return jnp.einsum('bths,bth->bts', qk, iw.astype(jnp.float32) * (IDX_HEADS ** -0.5))


def sparse_attend(q, ksel, vsel, valid):
    s = jnp.einsum('bthd,btkhd->bthk', q, ksel).astype(jnp.float32) * (HEAD_DIM ** -0.5)
    p = jax.nn.softmax(jnp.where(valid[:, :, None, :], s, -jnp.inf), axis=-1)
    return jnp.einsum('bthk,btkhd->bthd', p.astype(vsel.dtype), vsel)


def setup_inputs(seed: int = 0) -> dict:
    key = jax.random.key(seed)
    ks = iter(jax.random.split(key, 40))

    def nrm(shape, scale=1.0):
        return jax.random.normal(next(ks), shape, jnp.float32) * scale

    n_pages = PAST_LEN // PAGE_SIZE
    n_used = DEC_BATCH * n_pages
    n_pool = n_used + max(1, n_used // 4)
    page_table = jax.random.permutation(next(ks), n_pool)[:n_used].reshape(DEC_BATCH, n_pages).astype(jnp.int32)
    d = D_MODEL
    return {
        'x_prompt': nrm((BATCH, SEQ, d)),
        'x_sample': nrm((DEC_BATCH, DEC_SEQ, d)),
        'state_conv_mix': nrm((N_A, DEC_BATCH, CONV_W - 1, d)),
        'cache_sb_k': nrm((N_B, n_pool, PAGE_SIZE, N_HEADS, HEAD_DIM)),
        'cache_sb_v': nrm((N_B, n_pool, PAGE_SIZE, N_HEADS, HEAD_DIM)),
        'cache_diff_k': nrm((N_C, n_pool, PAGE_SIZE, DIFF_HEADS, 2, DIFF_DIM)),
        'cache_diff_v': nrm((N_C, n_pool, PAGE_SIZE, DIFF_HEADS, 2 * DIFF_DIM)),
        'cache_dsa_k': nrm((N_D, n_pool, PAGE_SIZE, N_HEADS, HEAD_DIM)),
        'cache_dsa_v': nrm((N_D, n_pool, PAGE_SIZE, N_HEADS, HEAD_DIM)),
        'cache_dsa_idx_k': nrm((N_D, n_pool, PAGE_SIZE, IDX_DIM)),
        'state_ffn_conv': nrm((DEPTH, DEC_BATCH, CONV_W - 1, 2 * D_FF)),
        'page_table': page_table,
        'g_mix': 1.0 + nrm((DEPTH, d), 0.02),
        'g_ffn': 1.0 + nrm((DEPTH, d), 0.02),
        'g_final': 1.0 + nrm((d,), 0.02),
        'w_a_in': nrm((N_A, d, 3 * d), d ** -0.5),
        'conv_a': nrm((N_A, CONV_W, d), CONV_W ** -0.5),
        'w_a_out': nrm((N_A, d, d), d ** -0.5),
        'w_b_qkv': nrm((N_B, d, 3 * d), d ** -0.5),
        'w_b_out': nrm((N_B, d, d), d ** -0.5),
        'w_c_qkv': nrm((N_C, d, 3 * d), d ** -0.5),
        'lam_c': nrm((N_C, 4, DIFF_DIM), 0.1),
        'g_c_subln': 1.0 + nrm((N_C, 2 * DIFF_DIM), 0.02),
        'w_c_out': nrm((N_C, d, d), d ** -0.5),
        'w_d_in': nrm((N_D, d, D_IN_DSA), d ** -0.5),
        'w_d_out': nrm((N_D, d, d), d ** -0.5),
        'w_ffn_up': nrm((DEPTH, d, 2 * D_FF), d ** -0.5),
        'conv_ffn': nrm((DEPTH, CONV_W, 2 * D_FF), CONV_W ** -0.5),
        'w_ffn_down': nrm((DEPTH, D_FF, d), D_FF ** -0.5),
    }


def reference(x_prompt, x_sample, state_conv_mix, cache_sb_k, cache_sb_v, cache_diff_k, cache_diff_v,
              cache_dsa_k, cache_dsa_v, cache_dsa_idx_k, state_ffn_conv, page_table,
              g_mix, g_ffn, g_final, w_a_in, conv_a, w_a_out, w_b_qkv, w_b_out,
              w_c_qkv, lam_c, g_c_subln, w_c_out, w_d_in, w_d_out, w_ffn_up, conv_ffn, w_ffn_down):
    xp, xs = x_prompt, x_sample
    pos_p = jnp.arange(SEQ)
    pos_s = PAST_LEN + jnp.arange(DEC_SEQ)
    kpos_s = jnp.arange(PAST_LEN + DEC_SEQ)
    incl_s = kpos_s[None, :] <= pos_s[:, None]
    strict_s = kpos_s[None, :] < pos_s[:, None]
    k_top_p = min(TOPK_MAX, SEQ // 4)
    k_top_s = min(TOPK_MAX, (PAST_LEN + DEC_SEQ) // 4)

    conv_p, conv_s = [], []
    sbk_p, sbk_s, sbv_p, sbv_s = [], [], [], []
    dk_p, dk_s, dv_p, dv_s = [], [], [], []
    ak_p, ak_s, av_p, av_s, ai_p, ai_s = [], [], [], [], [], []
    fc_p, fc_s = [], []

    for i in range(DEPTH):
        kind, j = i % N_MIXERS, i // N_MIXERS
        hp = rmsnorm(xp, g_mix[i])
        hs = rmsnorm(xs, g_mix[i])
        if kind == 0:
            zero_hist = jnp.zeros((BATCH, CONV_W - 1, D_MODEL), hp.dtype)
            mp, st_p = short_conv_mixer(hp, zero_hist, w_a_in[j], conv_a[j], w_a_out[j])
            ms, st_s = short_conv_mixer(hs, state_conv_mix[j], w_a_in[j], conv_a[j], w_a_out[j])
            conv_p.append(st_p)
            conv_s.append(st_s)
        elif kind == 1:
            qp, kp, vp = split_heads_qkv(hp @ w_b_qkv[j])

            def sb_block(qb, t0):
                t = t0 + jnp.arange(QBLOCK)
                return stick_breaking_attend(qb, kp, vp, pos_p[None, :] < t[:, None])

            op = map_query_blocks(sb_block, (qp,), SEQ)
            qs, ks_, vs_ = split_heads_qkv(hs @ w_b_qkv[j])
            k_all = jnp.concatenate([gather_pages(cache_sb_k, j, page_table).astype(ks_.dtype), ks_], axis=1)
            v_all = jnp.concatenate([gather_pages(cache_sb_v, j, page_table).astype(vs_.dtype), vs_], axis=1)
            os_ = stick_breaking_attend(qs, k_all, v_all, strict_s)
            mp = merge_heads(op) @ w_b_out[j]
            ms = merge_heads(os_) @ w_b_out[j]
            sbk_p.append(kp)
            sbk_s.append(ks_)
            sbv_p.append(vp)
            sbv_s.append(vs_)
        elif kind == 2:
            lambda_init = 0.8 - 0.6 * math.exp(-0.3 * i)
            lf = lam_c[j].astype(jnp.float32)
            lam = jnp.exp(jnp.sum(lf[0] * lf[1])) - jnp.exp(jnp.sum(lf[2] * lf[3])) + lambda_init
            qp, kp, vp = diff_project(hp, w_c_qkv[j], pos_p)

            def diff_block(qb, t0):
                t = t0 + jnp.arange(QBLOCK)
                return diff_attend(qb, kp, vp, pos_p[None, :] <= t[:, None], lam)

            op = map_query_blocks(diff_block, (qp,), SEQ)
            qs, ks_, vs_ = diff_project(hs, w_c_qkv[j], pos_s)
            k_all = jnp.concatenate([gather_pages(cache_diff_k, j, page_table).astype(ks_.dtype), ks_], axis=1)
            v_all = jnp.concatenate([gather_pages(cache_diff_v, j, page_table).astype(vs_.dtype), vs_], axis=1)
            os_ = diff_attend(qs, k_all, v_all, incl_s, lam)
            mp = merge_heads(rmsnorm(op, g_c_subln[j]) * (1.0 - lambda_init)) @ w_c_out[j]
            ms = merge_heads(rmsnorm(os_, g_c_subln[j]) * (1.0 - lambda_init)) @ w_c_out[j]
            dk_p.append(kp)
            dk_s.append(ks_)
            dv_p.append(vp)
            dv_s.append(vs_)
        else:
            qp, kp, vp, iqp, ikp, iwp = dsa_project(hp, w_d_in[j], pos_p)

            def dsa_block(qb, iqb, iwb, t0):
                t = t0 + jnp.arange(QBLOCK)
                sc = indexer_scores(iqb, iwb, ikp)
                sc = jnp.where(pos_p[None, None, :] <= t[None, :, None], sc, -jnp.inf)
                _, idx = lax.top_k(sc, k_top_p)
                ksel = take_rows(kp, idx)
                vsel = take_rows(vp, idx)
                return sparse_attend(qb, ksel, vsel, idx <= t[None, :, None])

            op = map_query_blocks(dsa_block, (qp, iqp, iwp), SEQ)
            qs, ks_, vs_, iqs, iks, iws = dsa_project(hs, w_d_in[j], pos_s)
            ik_all = jnp.concatenate([gather_pages(cache_dsa_idx_k, j, page_table).astype(iks.dtype), iks], axis=1)
            sc = jnp.where(incl_s[None], indexer_scores(iqs, iws, ik_all), -jnp.inf)
            _, idx = lax.top_k(sc, k_top_s)
            ksel = gather_selected(cache_dsa_k, j, page_table, ks_, idx).astype(ks_.dtype)
            vsel = gather_selected(cache_dsa_v, j, page_table, vs_, idx).astype(vs_.dtype)
            os_ = sparse_attend(qs, ksel, vsel, idx <= pos_s[None, :, None])
            mp = merge_heads(op) @ w_d_out[j]
            ms = merge_heads(os_) @ w_d_out[j]
            ak_p.append(kp)
            ak_s.append(ks_)
            av_p.append(vp)
            av_s.append(vs_)
            ai_p.append(ikp)
            ai_s.append(iks)
        xp = xp + mp
        xs = xs + ms
        fp_in = rmsnorm(xp, g_ffn[i])
        fs_in = rmsnorm(xs, g_ffn[i])
        zero_ffn = jnp.zeros((BATCH, CONV_W - 1, 2 * D_FF), fp_in.dtype)
        fp, fst_p = gated_conv_ffn(fp_in, zero_ffn, w_ffn_up[i], conv_ffn[i], w_ffn_down[i])
        fs, fst_s = gated_conv_ffn(fs_in, state_ffn_conv[i], w_ffn_up[i], conv_ffn[i], w_ffn_down[i])
        xp = xp + fp
        xs = xs + fs
        fc_p.append(fst_p)
        fc_s.append(fst_s)

    y_prompt = rmsnorm(xp, g_final)
    y_sample = rmsnorm(xs, g_final)
    conv_mix_p = jnp.stack(conv_p)
    conv_mix_s = jnp.stack(conv_s)
    sb_k_p = jnp.stack(sbk_p)
    sb_k_s = jnp.stack(sbk_s)
    sb_v_p = jnp.stack(sbv_p)
    sb_v_s = jnp.stack(sbv_s)
    diff_k_p = jnp.stack(dk_p)
    diff_k_s = jnp.stack(dk_s)
    diff_v_p = jnp.stack(dv_p)
    diff_v_s = jnp.stack(dv_s)
    dsa_k_p = jnp.stack(ak_p)
    dsa_k_s = jnp.stack(ak_s)
    dsa_v_p = jnp.stack(av_p)
    dsa_v_s = jnp.stack(av_s)
    dsa_idx_k_p = jnp.stack(ai_p)
    dsa_idx_k_s = jnp.stack(ai_s)
    ffn_conv_p = jnp.stack(fc_p)
    ffn_conv_s = jnp.stack(fc_s)
    return (y_prompt, y_sample, conv_mix_p, conv_mix_s, sb_k_p, sb_k_s, sb_v_p, sb_v_s,
            diff_k_p, diff_k_s, diff_v_p, diff_v_s, dsa_k_p, dsa_k_s, dsa_v_p, dsa_v_s,
            dsa_idx_k_p, dsa_idx_k_s, ffn_conv_p, ffn_conv_s)
```

```python
import functools
import math

import jax
import jax.numpy as jnp
from jax import lax
from jax.experimental import pallas as pl
from jax.experimental.pallas import tpu as pltpu

NORM_EPS = 1e-6
ROPE_THETA = 10000.0
TOPK_MAX = 256
CONV_W = 3
LANES = 128
SUBLANES = 8
VMEM_LIMIT_BYTES = 56 * 1024 * 1024
NEG_BIG = -1e30
INT_MIN = -2 ** 31

F32 = jnp.float32
BF16 = jnp.bfloat16


def _params(*sem):
    return pltpu.CompilerParams(dimension_semantics=sem, vmem_limit_bytes=VMEM_LIMIT_BYTES)


def _dot(a, b):
    return jnp.dot(a, b, preferred_element_type=F32)


def _dot_t(a, b):
    return lax.dot_general(a, b, (((1,), (1,)), ((), ())), preferred_element_type=F32)


def _rmsnorm_kernel(x_ref, g_ref, o_ref):
    x = x_ref[...]
    ms = jnp.mean(x * x, axis=-1, keepdims=True)
    o_ref[...] = ((x * lax.rsqrt(ms + NORM_EPS)) * g_ref[...]).astype(o_ref.dtype)


def _rmsnorm(x, g, out_dtype, tm):
    m, d = x.shape
    return pl.pallas_call(
        _rmsnorm_kernel,
        grid=(m // tm,),
        in_specs=[pl.BlockSpec((tm, d), lambda i: (i, 0)), pl.BlockSpec((1, d), lambda i: (0, 0))],
        out_specs=pl.BlockSpec((tm, d), lambda i: (i, 0)),
        out_shape=jax.ShapeDtypeStruct((m, d), out_dtype),
        compiler_params=_params("arbitrary"),
    )(x, g.reshape(1, d))


def _rope_group(y, c, s, half):
    if 2 * half == LANES:
        rot = pltpu.roll(y, half, 1)
    else:
        lane = lax.broadcasted_iota(jnp.int32, y.shape, 1)
        rot = jnp.where(lane % (2 * half) < half, pltpu.roll(y, LANES - half, 1), pltpu.roll(y, half, 1))
    return y * c + rot * s


def _proj_kernel(*refs, half, n_out):
    if half:
        x_ref, w_ref, c_ref, s_ref = refs[:4]
        outs = refs[4:4 + n_out]
    else:
        x_ref, w_ref = refs[:2]
        outs = refs[2:2 + n_out]
    wb_ref = refs[-1]

    @pl.when(pl.program_id(1) == 0)
    def _():
        wb_ref[...] = w_ref[...].astype(BF16)

    y = _dot(x_ref[...], wb_ref[...])
    tn = y.shape[1]
    if half:
        c = c_ref[...]
        s = s_ref[...]
        for g in range(tn // LANES):
            sl = slice(g * LANES, (g + 1) * LANES)
            yg = _rope_group(y[:, sl], c, s, half)
            for o in outs:
                o[:, sl] = yg.astype(o.dtype)
    else:
        for o in outs:
            o[...] = y.astype(o.dtype)


def _proj(xb, w, col0, ncols, out_dtypes, tm, rope=None):
    m, d = xb.shape
    tn = min(512, ncols)
    assert ncols % tn == 0 and col0 % tn == 0 and m % tm == 0
    off = col0 // tn
    in_specs = [pl.BlockSpec((tm, d), lambda j, i: (i, 0)),
                pl.BlockSpec((d, tn), lambda j, i: (0, j + off))]
    args = [xb, w]
    half = 0
    if rope is not None:
        ctab, stab, half, tblocks = rope
        in_specs += [pl.BlockSpec((tm, LANES), lambda j, i: (i % tblocks, 0)),
                     pl.BlockSpec((tm, LANES), lambda j, i: (i % tblocks, 0))]
        args += [ctab, stab]
    outs = pl.pallas_call(
        functools.partial(_proj_kernel, half=half, n_out=len(out_dtypes)),
        grid=(ncols // tn, m // tm),
        in_specs=in_specs,
        out_specs=[pl.BlockSpec((tm, tn), lambda j, i: (i, j)) for _ in out_dtypes],
        out_shape=[jax.ShapeDtypeStruct((m, ncols), dt) for dt in out_dtypes],
        scratch_shapes=[pltpu.VMEM((d, tn), BF16)],
        compiler_params=_params("arbitrary", "arbitrary"),
    )(*args)
    return outs


def _mm_res_kernel(a_ref, w_ref, r_ref, o_ref, wb_ref):
    @pl.when(pl.program_id(1) == 0)
    def _():
        wb_ref[...] = w_ref[...].astype(BF16)

    o_ref[...] = r_ref[...] + _dot(a_ref[...], wb_ref[...])


def _mm_res(a, w, res, tm):
    m, k = a.shape
    n = w.shape[1]
    tn = 512 if k <= 2048 else 256
    tn = min(tn, n)
    return pl.pallas_call(
        _mm_res_kernel,
        grid=(n // tn, m // tm),
        in_specs=[pl.BlockSpec((tm, k), lambda j, i: (i, 0)),
                  pl.BlockSpec((k, tn), lambda j, i: (0, j)),
                  pl.BlockSpec((tm, tn), lambda j, i: (i, j))],
        out_specs=pl.BlockSpec((tm, tn), lambda j, i: (i, j)),
        out_shape=jax.ShapeDtypeStruct((m, n), F32),
        scratch_shapes=[pltpu.VMEM((k, tn), BF16)],
        compiler_params=_params("arbitrary", "arbitrary"),
    )(a, w, res)


def _shifted(u, prev, period):
    row = lax.broadcasted_iota(jnp.int32, u.shape, 0) % period
    u1 = jnp.where(row >= 1, pltpu.roll(u, 1, 0), pltpu.roll(prev, 1, 0))
    u2 = jnp.where(row >= 2, pltpu.roll(u, 2, 0), pltpu.roll(prev, 2, 0))
    return u1, u2


def _dwconv(u, u1, u2, cw):
    return (cw[0:1, :] * u2 + cw[1:2, :] * u1) + cw[2:3, :] * u


def _history(prev_ref, hist_ref, tiles_per_seq):
    if hist_ref is not None:
        return hist_ref[...]

    @pl.when(pl.program_id(1) % tiles_per_seq == 0)
    def _():
        prev_ref[...] = jnp.zeros_like(prev_ref)

    return prev_ref[...]


def _convmix_kernel(*refs, short, tiles_per_seq, period):
    if short:
        x_ref, wb_ref, wc_ref, wh_ref, cw_ref, hist_ref, y_ref, st_ref, wbb, wcb, whb, prev_ref = refs
    else:
        x_ref, wb_ref, wc_ref, wh_ref, cw_ref, y_ref, st_ref, wbb, wcb, whb, prev_ref = refs
        hist_ref = None

    @pl.when(pl.program_id(1) == 0)
    def _():
        wbb[...] = wb_ref[...].astype(BF16)
        wcb[...] = wc_ref[...].astype(BF16)
        whb[...] = wh_ref[...].astype(BF16)

    x = x_ref[...]
    u = _dot(x, wcb[...]) * _dot(x, whb[...])
    prev = _history(prev_ref, hist_ref, tiles_per_seq)
    u1, u2 = _shifted(u, prev, period)
    conv = _dwconv(u, u1, u2, cw_ref[...])
    y_ref[...] = (_dot(x, wbb[...]) * conv).astype(y_ref.dtype)
    if short:
        st_ref[...] = u
    else:
        prev_ref[...] = u
        st_ref[...] = u[u.shape[0] - SUBLANES:, :]


def _state_specs(short, tm, tn, tiles_per_seq, col_off=0):
    if short:
        return pl.BlockSpec((tm, tn), lambda j, i: (0, j + col_off))
    return pl.BlockSpec((SUBLANES, tn), lambda j, i: (i // tiles_per_seq, j + col_off))


def _convmix(xb, w_in, conv_w, hist_x, tm, seq_len):
    m, d = xb.shape
    tn = min(512, d)
    nj = d // tn
    short = hist_x is not None
    if short:
        assert seq_len == SUBLANES and tm == m
        tiles_per_seq, period, nseq = 1, seq_len, m // seq_len
    else:
        assert seq_len % tm == 0
        tiles_per_seq, period, nseq = seq_len // tm, tm, m // seq_len
    in_specs = [pl.BlockSpec((tm, d), lambda j, i: (i, 0)),
                pl.BlockSpec((d, tn), lambda j, i: (0, j)),
                pl.BlockSpec((d, tn), lambda j, i: (0, j + nj)),
                pl.BlockSpec((d, tn), lambda j, i: (0, j + 2 * nj)),
                pl.BlockSpec((CONV_W, tn), lambda j, i: (0, j))]
    args = [xb, w_in, w_in, w_in, conv_w]
    if short:
        in_specs.append(pl.BlockSpec((tm, tn), lambda j, i: (0, j)))
        args.append(hist_x)
    y, st = pl.pallas_call(
        functools.partial(_convmix_kernel, short=short, tiles_per_seq=tiles_per_seq, period=period),
        grid=(nj, m // tm),
        in_specs=in_specs,
        out_specs=[pl.BlockSpec((tm, tn), lambda j, i: (i, j)), _state_specs(short, tm, tn, tiles_per_seq)],
        out_shape=[jax.ShapeDtypeStruct((m, d), BF16), jax.ShapeDtypeStruct((nseq * SUBLANES, d), F32)],
        scratch_shapes=[pltpu.VMEM((d, tn), BF16)] * 3 + [pltpu.VMEM((tm, tn), F32)],
        compiler_params=_params("arbitrary", "arbitrary"),
    )(*args)
    return y, st


def _ffn_up_kernel(*refs, short, tiles_per_seq, period):
    if short:
        (x_ref, wg_ref, wu_ref, cg_ref, cu_ref, hg_ref, hu_ref,
         a_ref, sg_ref, su_ref, wgb, wub, pg_ref, pu_ref) = refs
    else:
        x_ref, wg_ref, wu_ref, cg_ref, cu_ref, a_ref, sg_ref, su_ref, wgb, wub, pg_ref, pu_ref = refs
        hg_ref = hu_ref = None

    @pl.when(pl.program_id(1) == 0)
    def _():
        wgb[...] = wg_ref[...].astype(BF16)
        wub[...] = wu_ref[...].astype(BF16)

    x = x_ref[...]
    halves = []
    for w_b, c_ref, h_ref, p_ref, s_ref in ((wgb, cg_ref, hg_ref, pg_ref, sg_ref),
                                            (wub, cu_ref, hu_ref, pu_ref, su_ref)):
        up = _dot(x, w_b[...])
        prev = _history(p_ref, h_ref, tiles_per_seq)
        u1, u2 = _shifted(up, prev, period)
        halves.append(_dwconv(up, u1, u2, c_ref[...]))
        if short:
            s_ref[...] = up
        else:
            p_ref[...] = up
            s_ref[...] = up[up.shape[0] - SUBLANES:, :]
    g, u = halves
    a_ref[...] = ((g * jax.nn.sigmoid(g)) * u).astype(a_ref.dtype)


def _ffn_up(xb, w_up, conv_w, hist_x, tm, seq_len):
    m, d = xb.shape
    d_ff = w_up.shape[1] // 2
    tn = 512 if d_ff % 512 == 0 else LANES
    nj = d_ff // tn
    short = hist_x is not None
    if short:
        assert seq_len == SUBLANES and tm == m
        tiles_per_seq, period, nseq = 1, seq_len, m // seq_len
    else:
        assert seq_len % tm == 0
        tiles_per_seq, period, nseq = seq_len // tm, tm, m // seq_len
    in_specs = [pl.BlockSpec((tm, d), lambda j, i: (i, 0)),
                pl.BlockSpec((d, tn), lambda j, i: (0, j)),
                pl.BlockSpec((d, tn), lambda j, i: (0, j + nj)),
                pl.BlockSpec((CONV_W, tn), lambda j, i: (0, j)),
                pl.BlockSpec((CONV_W, tn), lambda j, i: (0, j + nj))]
    args = [xb, w_up, w_up, conv_w, conv_w]
    if short:
        in_specs += [pl.BlockSpec((tm, tn), lambda j, i: (0, j)), pl.BlockSpec((tm, tn), lambda j, i: (0, j + nj))]
        args += [hist_x, hist_x]
    act, sg, su = pl.pallas_call(
        functools.partial(_ffn_up_kernel, short=short, tiles_per_seq=tiles_per_seq, period=period),
        grid=(nj, m // tm),
        in_specs=in_specs,
        out_specs=[pl.BlockSpec((tm, tn), lambda j, i: (i, j)),
                   _state_specs(short, tm, tn, tiles_per_seq),
                   _state_specs(short, tm, tn, tiles_per_seq)],
        out_shape=[jax.ShapeDtypeStruct((m, d_ff), BF16),
                   jax.ShapeDtypeStruct((nseq * SUBLANES, d_ff), F32),
                   jax.ShapeDtypeStruct((nseq * SUBLANES, d_ff), F32)],
        scratch_shapes=[pltpu.VMEM((d, tn), BF16)] * 2 + [pltpu.VMEM((tm, tn), F32)] * 2,
        compiler_params=_params("arbitrary", "arbitrary"),
    )(*args)
    return act, jnp.concatenate([sg, su], axis=1)


def _log_sigmoid_pair(z):
    l1p = jnp.log(1.0 + jnp.exp(-jnp.abs(z)))
    ls = jnp.minimum(z, 0.0) - l1p
    return ls, ls - z


def _suffix_sum(x, tri):
    hi = x.astype(BF16)
    lo = (x - hi.astype(F32)).astype(BF16)
    return _dot(hi, tri) + _dot(lo, tri)


def _tri(n):
    j = lax.broadcasted_iota(jnp.int32, (n, n), 0)
    s = lax.broadcasted_iota(jnp.int32, (n, n), 1)
    return (j > s).astype(BF16)


def _sb_prompt_kernel(q_ref, k_ref, v_ref, tri_ref, o_ref, *, tq, scale):
    qi = pl.program_id(2)
    q = q_ref[...]
    tri = tri_ref[...]

    def block(ki, carry, acc, diag):
        start = pl.multiple_of(ki * tq, tq)
        k = k_ref[pl.ds(start, tq), :]
        v = v_ref[pl.ds(start, tq), :]
        z = _dot_t(q, k) * scale
        ls, l1m = _log_sigmoid_pair(z)
        if diag:
            row = lax.broadcasted_iota(jnp.int32, z.shape, 0)
            col = lax.broadcasted_iota(jnp.int32, z.shape, 1)
            valid = col < row
            l1m = jnp.where(valid, l1m, 0.0)
        after = _suffix_sum(l1m, tri)
        a = jnp.exp(ls + after + carry)
        if diag:
            a = jnp.where(valid, a, 0.0)
        acc = acc + _dot(a.astype(BF16), v)
        carry = carry + jnp.sum(l1m, axis=1, keepdims=True)
        return carry, acc

    carry0 = jnp.zeros((tq, 1), F32)
    acc0 = jnp.zeros((tq, v_ref.shape[1]), F32)
    carry, acc = block(qi, carry0, acc0, True)

    def body(it, c):
        return block(qi - 1 - it, c[0], c[1], False)

    carry, acc = lax.fori_loop(0, qi, body, (carry, acc))
    o_ref[...] = acc.astype(o_ref.dtype)


def _sb_prompt(qb, kb, vb, batch, seq, n_heads, hd):
    m, d = qb.shape
    tq = min(256, seq)
    nq = seq // tq
    return pl.pallas_call(
        functools.partial(_sb_prompt_kernel, tq=tq, scale=hd ** -0.5),
        grid=(batch, n_heads, nq),
        in_specs=[pl.BlockSpec((tq, hd), lambda b, h, q: (b * nq + q, h)),
                  pl.BlockSpec((seq, hd), lambda b, h, q: (b, h)),
                  pl.BlockSpec((seq, hd), lambda b, h, q: (b, h)),
                  pl.BlockSpec((tq, tq), lambda b, h, q: (0, 0))],
        out_specs=pl.BlockSpec((tq, hd), lambda b, h, q: (b * nq + q, h)),
        out_shape=jax.ShapeDtypeStruct((m, d), BF16),
        compiler_params=_params("arbitrary", "arbitrary", "arbitrary"),
    )(qb, kb, vb, _tri(tq))


def _online_softmax_step(s, mask, m, l, acc, v):
    if mask is not None:
        s = jnp.where(mask, s, NEG_BIG)
    m_new = jnp.maximum(m, jnp.max(s, axis=1, keepdims=True))
    p = jnp.exp(s - m_new)
    if mask is not None:
        p = jnp.where(mask, p, 0.0)
    alpha = jnp.exp(m - m_new)
    l = alpha * l + jnp.sum(p, axis=1, keepdims=True)
    acc = alpha * acc + _dot(p.astype(BF16), v)
    return m_new, l, acc


def _lambda_full(lam_ref, lambda_init):
    lf = lam_ref[...]
    s01 = jnp.sum(lf[0:1, :] * lf[1:2, :], axis=1, keepdims=True)
    s23 = jnp.sum(lf[2:3, :] * lf[3:4, :], axis=1, keepdims=True)
    return jnp.exp(s01) - jnp.exp(s23) + lambda_init


def _sub_norm(o, g, lambda_init):
    ms = jnp.mean(o * o, axis=-1, keepdims=True)
    return ((o * lax.rsqrt(ms + NORM_EPS)) * g) * (1.0 - lambda_init)


def _diff_prompt_kernel(q_ref, k_ref, v_ref, lam_ref, g_ref, o_ref, *, tq, hd, scale, lambda_init):
    qi = pl.program_id(2)
    dv = v_ref.shape[1]

    def block(ki, state, diag):
        start = pl.multiple_of(ki * tq, tq)
        v = v_ref[pl.ds(start, tq), :]
        mask = None
        if diag:
            row = lax.broadcasted_iota(jnp.int32, (tq, tq), 0)
            col = lax.broadcasted_iota(jnp.int32, (tq, tq), 1)
            mask = col <= row
        new = []
        for i in range(2):
            q = q_ref[:, i * hd:(i + 1) * hd]
            k = k_ref[pl.ds(start, tq), i * hd:(i + 1) * hd]
            s = _dot_t(q, k) * scale
            new.append(_online_softmax_step(s, mask, *state[i], v))
        return tuple(new)

    init = tuple((jnp.full((tq, 1), NEG_BIG, F32), jnp.zeros((tq, 1), F32), jnp.zeros((tq, dv), F32))
                 for _ in range(2))
    state = lax.fori_loop(0, qi, lambda ki, st: block(ki, st, False), init)
    state = block(qi, state, True)
    lam = _lambda_full(lam_ref, lambda_init)
    (_, l0, a0), (_, l1, a1) = state
    o = a0 / l0 - lam * (a1 / l1)
    o_ref[...] = _sub_norm(o, g_ref[...], lambda_init).astype(o_ref.dtype)


def _diff_prompt(qb, kb, vb, lam, g_sub, batch, seq, n_dh, hd, lambda_init):
    m, d = qb.shape
    tq = min(256, seq)
    nq = seq // tq
    w = 2 * hd
    return pl.pallas_call(
        functools.partial(_diff_prompt_kernel, tq=tq, hd=hd, scale=hd ** -0.5, lambda_init=lambda_init),
        grid=(batch, n_dh, nq),
        in_specs=[pl.BlockSpec((tq, w), lambda b, h, q: (b * nq + q, h)),
                  pl.BlockSpec((seq, w), lambda b, h, q: (b, h)),
                  pl.BlockSpec((seq, w), lambda b, h, q: (b, h)),
                  pl.BlockSpec(lam.shape, lambda b, h, q: (0, 0)),
                  pl.BlockSpec((1, w), lambda b, h, q: (0, 0))],
        out_specs=pl.BlockSpec((tq, w), lambda b, h, q: (b * nq + q, h)),
        out_shape=jax.ShapeDtypeStruct((m, d), BF16),
        compiler_params=_params("arbitrary", "arbitrary", "arbitrary"),
    )(qb, kb, vb, lam, g_sub.reshape(1, w))


def _dsa_prompt_kernel(q_ref, k_ref, v_ref, sel_ref, o_ref, *, tq, scale):
    qi = pl.program_id(2)
    q = q_ref[...]
    dv = v_ref.shape[1]

    def block(ki, state):
        start = pl.multiple_of(ki * tq, tq)
        k = k_ref[pl.ds(start, tq), :]
        v = v_ref[pl.ds(start, tq), :]
        mask = sel_ref[:, pl.ds(start, tq)].astype(F32) > 0.5
        s = _dot_t(q, k) * scale
        return _online_softmax_step(s, mask, *state, v)

    init = (jnp.full((tq, 1), NEG_BIG, F32), jnp.zeros((tq, 1), F32), jnp.zeros((tq, dv), F32))
    _, l, acc = lax.fori_loop(0, qi + 1, block, init)
    o_ref[...] = (acc / l).astype(o_ref.dtype)


def _dsa_prompt(qb, kb, vb, sel, batch, seq, n_heads, hd):
    m, d = qb.shape
    tq = min(256, seq)
    nq = seq // tq
    return pl.pallas_call(
        functools.partial(_dsa_prompt_kernel, tq=tq, scale=hd ** -0.5),
        grid=(batch, n_heads, nq),
        in_specs=[pl.BlockSpec((tq, hd), lambda b, h, q: (b * nq + q, h)),
                  pl.BlockSpec((seq, hd), lambda b, h, q: (b, h)),
                  pl.BlockSpec((seq, hd), lambda b, h, q: (b, h)),
                  pl.BlockSpec((tq, seq), lambda b, h, q: (b * nq + q, 0))],
        out_specs=pl.BlockSpec((tq, hd), lambda b, h, q: (b * nq + q, h)),
        out_shape=jax.ShapeDtypeStruct((m, d), BF16),
        compiler_params=_params("arbitrary", "arbitrary", "arbitrary"),
    )(qb, kb, vb, sel)


def _order_key(x):
    bits = pltpu.bitcast(x + 0.0, jnp.int32)
    return jnp.where(bits >= 0, bits, bits ^ jnp.int32(0x7FFFFFFF))


def _kth_largest_key(key, k):
    def count_ge(t):
        return jnp.sum((key >= t).astype(jnp.int32), axis=1, keepdims=True)

    t0 = jnp.where(count_ge(jnp.int32(0)) >= k, jnp.int32(0), jnp.int32(INT_MIN))
    t0 = jnp.broadcast_to(t0, (key.shape[0], 1))

    def body(it, t):
        cand = t | jnp.left_shift(jnp.int32(1), jnp.int32(30) - it)
        return jnp.where(count_ge(cand) >= k, cand, t)

    return lax.fori_loop(0, 31, body, t0)


def _store_topk_mask(o_ref, key, valid, k):
    t = _kth_largest_key(key, k)
    ge = valid & (key >= t)
    n_ge = jnp.sum(ge.astype(jnp.int32), axis=1, keepdims=True)
    has_ties = jnp.max(n_ge) > k

    @pl.when(jnp.logical_not(has_ties))
    def _():
        o_ref[...] = jnp.where(ge, 1.0, 0.0).astype(o_ref.dtype)

    @pl.when(has_ties)
    def _():
        gt = key > t
        eq = key == t
        need = k - jnp.sum(gt.astype(jnp.int32), axis=1, keepdims=True)
        col = lax.broadcasted_iota(jnp.int32, key.shape, 1)
        nbits = int(key.shape[1]).bit_length()

        def body(it, j):
            cand = j | jnp.left_shift(jnp.int32(1), jnp.int32(nbits - 1) - it)
            c = jnp.sum((eq & (col < cand)).astype(jnp.int32), axis=1, keepdims=True)
            return jnp.where(c < need, cand, j)

        j = lax.fori_loop(0, nbits, body, jnp.zeros_like(t))
        sel = valid & (gt | (eq & (col <= j)))
        o_ref[...] = jnp.where(sel, 1.0, 0.0).astype(o_ref.dtype)


def _select_prompt_kernel(iq_ref, ik_ref, iw_ref, o_ref, sc_ref, *, tq, n_ih, idim, k_top, wscale):
    qi = pl.program_id(1)
    ik = ik_ref[...]
    iw = iw_ref[...]
    lane = lax.broadcasted_iota(jnp.int32, (tq, LANES), 1)
    per_group = LANES // idim
    for h in range(n_ih):
        g, r = divmod(h, per_group)
        qg = iq_ref[:, g * LANES:(g + 1) * LANES].astype(F32)
        qh = jnp.where((lane >= r * idim) & (lane < (r + 1) * idim), qg, 0.0).astype(BF16)
        s = jnp.maximum(_dot_t(qh, ik), 0.0) * (iw[:, idim + h:idim + h + 1] * wscale)
        if h == 0:
            sc_ref[...] = s
        else:
            sc_ref[...] += s
    seq = sc_ref.shape[1]
    row = qi * tq + lax.broadcasted_iota(jnp.int32, (tq, seq), 0)
    col = lax.broadcasted_iota(jnp.int32, (tq, seq), 1)
    causal = col <= row
    key = _order_key(jnp.where(causal, sc_ref[...], -jnp.inf))
    _store_topk_mask(o_ref, key, causal, k_top)


def _select_prompt(iqb, ik2b, tail, batch, seq, n_ih, idim, k_top):
    m = iqb.shape[0]
    tq = min(128, seq)
    nq = seq // tq
    assert LANES % idim == 0
    return pl.pallas_call(
        functools.partial(_select_prompt_kernel, tq=tq, n_ih=n_ih, idim=idim, k_top=k_top,
                          wscale=(idim ** -0.5) * (n_ih ** -0.5)),
        grid=(batch, nq),
        in_specs=[pl.BlockSpec((tq, n_ih * idim), lambda b, q: (b * nq + q, 0)),
                  pl.BlockSpec((seq, LANES), lambda b, q: (b, 0)),
                  pl.BlockSpec((tq, LANES), lambda b, q: (b * nq + q, 0))],
        out_specs=pl.BlockSpec((tq, seq), lambda b, q: (b * nq + q, 0)),
        out_shape=jax.ShapeDtypeStruct((m, seq), BF16),
        scratch_shapes=[pltpu.VMEM((tq, seq), F32)],
        compiler_params=_params("arbitrary", "arbitrary"),
    )(iqb, ik2b, tail)


def _sel_sample_kernel(pt_ref, qm_ref, w_ref, iknew_ref, *refs, pages_per_step, n_pages, ds, k_top):
    ik_refs = refs[:pages_per_step]
    o_ref = refs[pages_per_step]
    sc_ref = refs[pages_per_step + 1]
    s_id = pl.program_id(1)
    qm = qm_ref[...]
    w = w_ref[...]
    page = ik_refs[0].shape[0]

    def scores(ik):
        s = jnp.maximum(_dot_t(qm, ik.astype(BF16)), 0.0) * w
        tot = s[0:ds, :]
        for h in range(1, s.shape[0] // ds):
            tot = tot + s[h * ds:(h + 1) * ds, :]
        return tot

    for c in range(pages_per_step):
        pg = s_id * pages_per_step + c
        sc_ref[:, pl.ds(pl.multiple_of(pg * page, page), page)] = scores(ik_refs[c][...])

    @pl.when(s_id == pl.num_programs(1) - 1)
    def _():
        t = lax.broadcasted_iota(jnp.int32, (ds, page), 0)
        j = lax.broadcasted_iota(jnp.int32, (ds, page), 1)
        sc_ref[:, n_pages * page:] = jnp.where(j <= t, scores(iknew_ref[...]), -jnp.inf)
        col = lax.broadcasted_iota(jnp.int32, sc_ref.shape, 1)
        row = lax.broadcasted_iota(jnp.int32, sc_ref.shape, 0)
        _store_topk_mask(o_ref, _order_key(sc_ref[...]), col <= n_pages * page + row, k_top)


def _select_sample(page_table, qm, wcol, iknew, pool, n_pages, k_top):
    db, rows, idim = qm.shape
    page = pool.shape[1]
    ds = SUBLANES
    pps = 8 if n_pages % 8 == 0 else (4 if n_pages % 4 == 0 else 1)
    n_steps = n_pages // pps
    nk = (n_pages + 1) * page

    def pool_spec(c):
        return pl.BlockSpec((None, page, idim), lambda b, s, pt: (pt[b, s * pps + c], 0, 0))

    grid_spec = pltpu.PrefetchScalarGridSpec(
        num_scalar_prefetch=1,
        grid=(db, n_steps),
        in_specs=[pl.BlockSpec((None, rows, idim), lambda b, s, pt: (b, 0, 0)),
                  pl.BlockSpec((None, rows, 1), lambda b, s, pt: (b, 0, 0)),
                  pl.BlockSpec((None, page, idim), lambda b, s, pt: (b, 0, 0))]
                 + [pool_spec(c) for c in range(pps)],
        out_specs=pl.BlockSpec((None, ds, nk), lambda b, s, pt: (b, 0, 0)),
        scratch_shapes=[pltpu.VMEM((ds, nk), F32)],
    )
    return pl.pallas_call(
        functools.partial(_sel_sample_kernel, pages_per_step=pps, n_pages=n_pages, ds=ds, k_top=k_top),
        grid_spec=grid_spec,
        out_shape=jax.ShapeDtypeStruct((db, ds, nk), BF16),
        compiler_params=_params("arbitrary", "arbitrary"),
    )(page_table, qm, wcol, iknew, *([pool] * pps))


def _paged_kernel(pt_ref, *refs, kind, pages_per_step, n_heads, ds, scale, lambda_init):
    n_in = 3 + (2 if kind == "dsa" else 0) + 2 * pages_per_step + (2 if kind == "diff" else 0) + (1 if kind == "sb" else 0)
    ins, o_ref, (qbd_ref, acc_ref, m_ref, l_ref) = refs[:n_in], refs[n_in], refs[n_in + 1:]
    q_ref, knew_ref, vnew_ref = ins[:3]
    pos = 3
    if kind == "dsa":
        selnew_ref, sel_ref = ins[pos:pos + 2]
        pos += 2
    k_refs = ins[pos:pos + pages_per_step]
    v_refs = ins[pos + pages_per_step:pos + 2 * pages_per_step]
    pos += 2 * pages_per_step
    if kind == "diff":
        lam_ref, g_ref = ins[pos:pos + 2]
    if kind == "sb":
        tri_ref = ins[pos]
    s_id = pl.program_id(1)
    rows = n_heads * ds
    page = knew_ref.shape[0]
    d = knew_ref.shape[1]
    hd = d // n_heads

    def process(k, v, mask):
        sc = _dot_t(qbd_ref[...], k.astype(BF16)) * scale
        vb = v.astype(BF16)
        if kind == "sb":
            ls, l1m = _log_sigmoid_pair(sc)
            if mask is not None:
                l1m = jnp.where(mask, l1m, 0.0)
            after = _suffix_sum(l1m, tri_ref[...])
            a = jnp.exp(ls + after + m_ref[...])
            if mask is not None:
                a = jnp.where(mask, a, 0.0)
            acc_ref[...] += _dot(a.astype(BF16), vb)
            m_ref[...] += jnp.sum(l1m, axis=1, keepdims=True)
        else:
            m_new, l_new, acc_new = _online_softmax_step(sc, mask, m_ref[...], l_ref[...], acc_ref[...], vb)
            m_ref[...] = m_new
            l_ref[...] = l_new
            acc_ref[...] = acc_new

    def tile_rows(x):
        return jnp.concatenate([x] * n_heads, axis=0)

    @pl.when(s_id == 0)
    def _():
        qt = tile_rows(q_ref[...].astype(F32))
        rh = lax.broadcasted_iota(jnp.int32, (rows, d), 0) // ds
        ch = lax.broadcasted_iota(jnp.int32, (rows, d), 1) // hd
        qbd_ref[...] = jnp.where(rh == ch, qt, 0.0).astype(BF16)
        acc_ref[...] = jnp.zeros_like(acc_ref)
        l_ref[...] = jnp.zeros_like(l_ref)
        m_ref[...] = jnp.zeros_like(m_ref) if kind == "sb" else jnp.full_like(m_ref, NEG_BIG)
        t = lax.broadcasted_iota(jnp.int32, (rows, page), 0) % ds
        j = lax.broadcasted_iota(jnp.int32, (rows, page), 1)
        if kind == "sb":
            mask = j < t
        elif kind == "diff":
            mask = j <= t
        else:
            mask = tile_rows(selnew_ref[...].astype(F32)) > 0.5
        process(knew_ref[...], vnew_ref[...], mask)

    for c in range(pages_per_step):
        mask = None
        if kind == "dsa":
            lo = (pages_per_step - 1 - c) * page
            mask = tile_rows(sel_ref[:, lo:lo + page].astype(F32)) > 0.5
        process(k_refs[c][...], v_refs[c][...], mask)

    @pl.when(s_id == pl.num_programs(1) - 1)
    def _():
        acc = acc_ref[...]
        if kind == "diff":
            lam = _lambda_full(lam_ref, lambda_init)
            inv = 1.0 / l_ref[...]
            w = 2 * hd
            for n in range(n_heads // 2):
                a0 = acc[(2 * n) * ds:(2 * n + 1) * ds, n * w:(n + 1) * w] * inv[(2 * n) * ds:(2 * n + 1) * ds, :]
                a1 = acc[(2 * n + 1) * ds:(2 * n + 2) * ds, n * w:(n + 1) * w] * inv[(2 * n + 1) * ds:(2 * n + 2) * ds, :]
                o_ref[:, n * w:(n + 1) * w] = _sub_norm(a0 - lam * a1, g_ref[...], lambda_init).astype(o_ref.dtype)
        else:
            if kind == "dsa":
                acc = acc / l_ref[...]
            for h in range(n_heads):
                o_ref[:, h * hd:(h + 1) * hd] = acc[h * ds:(h + 1) * ds, h * hd:(h + 1) * hd].astype(o_ref.dtype)


def _paged_attention(kind, page_table, q, knew, vnew, kpool, vpool, n_heads, n_pages, *,
                     sel=None, lam=None, g_sub=None, lambda_init=0.0):
    db, ds, d = q.shape
    page = kpool.shape[1]
    hd = d // n_heads
    pps = 4 if n_pages % 4 == 0 else 1
    n_steps = n_pages // pps
    rows = n_heads * ds

    def pool_spec(c):
        return pl.BlockSpec((None, page, d), lambda b, s, pt: (pt[b, n_pages - 1 - (s * pps + c)], 0, 0))

    in_specs = [pl.BlockSpec((None, ds, d), lambda b, s, pt: (b, 0, 0)),
                pl.BlockSpec((None, page, d), lambda b, s, pt: (b, 0, 0)),
                pl.BlockSpec((None, page, d), lambda b, s, pt: (b, 0, 0))]
    args = [q, knew, vnew]
    if kind == "dsa":
        in_specs += [pl.BlockSpec((None, ds, page), lambda b, s, pt: (b, 0, n_pages)),
                     pl.BlockSpec((None, ds, pps * page), lambda b, s, pt: (b, 0, n_steps - 1 - s))]
        args += [sel, sel]
    in_specs += [pool_spec(c) for c in range(pps)] * 2
    args += [kpool] * pps + [vpool] * pps
    if kind == "diff":
        in_specs += [pl.BlockSpec(lam.shape, lambda b, s, pt: (0, 0)),
                     pl.BlockSpec((1, 2 * hd), lambda b, s, pt: (0, 0))]
        args += [lam, g_sub.reshape(1, 2 * hd)]
    if kind == "sb":
        in_specs.append(pl.BlockSpec((page, page), lambda b, s, pt: (0, 0)))
        args.append(_tri(page))
    grid_spec = pltpu.PrefetchScalarGridSpec(
        num_scalar_prefetch=1,
        grid=(db, n_steps),
        in_specs=in_specs,
        out_specs=pl.BlockSpec((None, ds, d), lambda b, s, pt: (b, 0, 0)),
        scratch_shapes=[pltpu.VMEM((rows, d), BF16), pltpu.VMEM((rows, d), F32),
                        pltpu.VMEM((rows, 1), F32), pltpu.VMEM((rows, 1), F32)],
    )
    return pl.pallas_call(
        functools.partial(_paged_kernel, kind=kind, pages_per_step=pps, n_heads=n_heads, ds=ds,
                          scale=hd ** -0.5, lambda_init=lambda_init),
        grid_spec=grid_spec,
        out_shape=jax.ShapeDtypeStruct((db, ds, d), BF16),
        compiler_params=_params("arbitrary", "arbitrary"),
    )(page_table, *args)


def _rope_tables(pos, half, ident_from=None):
    inv = ROPE_THETA ** (-jnp.arange(half, dtype=F32) / half)
    ang = pos.astype(F32)[:, None] * inv
    cos, sin = jnp.cos(ang), jnp.sin(ang)
    reps = LANES // (2 * half)
    c = jnp.tile(jnp.concatenate([cos, cos], axis=1), (1, reps))
    s = jnp.tile(jnp.concatenate([-sin, sin], axis=1), (1, reps))
    if ident_from is not None:
        lane = jnp.arange(LANES)[None, :]
        c = jnp.where(lane < ident_from, c, 1.0)
        s = jnp.where(lane < ident_from, s, 0.0)
    return c, s


def _expand_history(hist):
    nseq, _, c = hist.shape
    z = jnp.zeros((nseq, SUBLANES - 2, c), hist.dtype)
    blocks = jnp.concatenate([z, hist], axis=1).reshape(nseq * SUBLANES, c)
    return jnp.roll(blocks, -SUBLANES, axis=0)


def _last_two(st, nseq):
    return st.reshape(nseq, SUBLANES, st.shape[1])[:, SUBLANES - 2:, :]


def _pad_rows(x, rows):
    return jnp.pad(x, ((0, 0), (0, rows - x.shape[1]), (0, 0)))


def kernel(x_prompt, x_sample, state_conv_mix, cache_sb_k, cache_sb_v, cache_diff_k, cache_diff_v, cache_dsa_k, cache_dsa_v, cache_dsa_idx_k, state_ffn_conv, page_table, g_mix, g_ffn, g_final, w_a_in, conv_a, w_a_out, w_b_qkv, w_b_out, w_c_qkv, lam_c, g_c_subln, w_c_out, w_d_in, w_d_out, w_ffn_up, conv_ffn, w_ffn_down):
    batch, seq, d = x_prompt.shape
    db, ds, _ = x_sample.shape
    depth = g_mix.shape[0]
    n_pool, page = cache_sb_k.shape[1], cache_sb_k.shape[2]
    n_heads, hd = cache_sb_k.shape[3], cache_sb_k.shape[4]
    n_dh = cache_diff_k.shape[3]
    idim = cache_dsa_idx_k.shape[3]
    n_pages = page_table.shape[1]
    past = n_pages * page
    n_ih = (w_d_in.shape[2] - 3 * d - idim) // (idim + 1)
    assert ds == SUBLANES and hd == LANES and cache_diff_k.shape[5] == hd and n_dh * 2 == n_heads
    assert (n_ih * idim) % LANES == 0 and idim + n_ih <= LANES

    mp, ms = batch * seq, db * ds
    tm_p = min(512, seq)
    streams = (("p", tm_p, seq), ("s", ms, ds))
    x = {"p": x_prompt.reshape(mp, d), "s": x_sample.reshape(ms, d)}
    pos = {"p": jnp.arange(seq), "s": past + jnp.arange(ds)}
    pos_rows = {"p": pos["p"], "s": jnp.tile(pos["s"], db)}
    rope_blocks = {"p": seq // tm_p, "s": 1}
    rope_hd = {n: _rope_tables(pos_rows[n], hd // 2) for n in x}
    rope_idx = {n: _rope_tables(pos_rows[n], idim // 2) for n in x}
    rope_tail = {n: _rope_tables(pos_rows[n], idim // 2, ident_from=idim) for n in x}
    k_top = {"p": min(TOPK_MAX, seq // 4), "s": min(TOPK_MAX, (past + ds) // 4)}
    nseq = {"p": batch, "s": db}

    outs = {name: {"p": [], "s": []} for name in
            ("conv", "sbk", "sbv", "dk", "dv", "ak", "av", "ai", "fc")}

    def shaped(a, n, tail):
        return a.reshape((nseq[n], seq if n == "p" else ds) + tail)

    for i in range(depth):
        kind, j = i % 4, i // 4
        for n, tm, slen in streams:
            xb = _rmsnorm(x[n], g_mix[i], BF16, tm)
            if kind == 0:
                hist = None if n == "p" else _expand_history(state_conv_mix[j])
                y, st = _convmix(xb, w_a_in[j], conv_a[j], hist, tm, slen)
                outs["conv"][n].append(_last_two(st, nseq[n]))
                w_out = w_a_out[j]
            elif kind == 1:
                w = w_b_qkv[j]
                (qb,) = _proj(xb, w, 0, d, (BF16,), tm)
                kf, kb = _proj(xb, w, d, d, (F32, BF16), tm)
                vf, vb = _proj(xb, w, 2 * d, d, (F32, BF16), tm)
                outs["sbk"][n].append(shaped(kf, n, (n_heads, hd)))
                outs["sbv"][n].append(shaped(vf, n, (n_heads, hd)))
                if n == "p":
                    y = _sb_prompt(qb, kb, vb, batch, seq, n_heads, hd)
                else:
                    y = _paged_attention(
                        "sb", page_table, qb.reshape(db, ds, d),
                        _pad_rows(kf.reshape(db, ds, d), page), _pad_rows(vf.reshape(db, ds, d), page),
                        cache_sb_k[j].reshape(n_pool, page, d), cache_sb_v[j].reshape(n_pool, page, d),
                        n_heads, n_pages).reshape(ms, d)
                w_out = w_b_out[j]
            elif kind == 2:
                lambda_init = 0.8 - 0.6 * math.exp(-0.3 * i)
                w = w_c_qkv[j]
                rope = rope_hd[n] + (hd // 2, rope_blocks[n])
                (qb,) = _proj(xb, w, 0, d, (BF16,), tm, rope)
                kf, kb = _proj(xb, w, d, d, (F32, BF16), tm, rope)
                vf, vb = _proj(xb, w, 2 * d, d, (F32, BF16), tm)
                outs["dk"][n].append(shaped(kf, n, (n_dh, 2, hd)))
                outs["dv"][n].append(shaped(vf, n, (n_dh, 2 * hd)))
                if n == "p":
                    y = _diff_prompt(qb, kb, vb, lam_c[j], g_c_subln[j], batch, seq, n_dh, hd, lambda_init)
                else:
                    y = _paged_attention(
                        "diff", page_table, qb.reshape(db, ds, d),
                        _pad_rows(kf.reshape(db, ds, d), page), _pad_rows(vf.reshape(db, ds, d), page),
                        cache_diff_k[j].reshape(n_pool, page, d), cache_diff_v[j].reshape(n_pool, page, d),
                        n_heads, n_pages, lam=lam_c[j], g_sub=g_c_subln[j], lambda_init=lambda_init).reshape(ms, d)
                w_out = w_c_out[j]
            else:
                w = w_d_in[j]
                rope = rope_hd[n] + (hd // 2, rope_blocks[n])
                (qb,) = _proj(xb, w, 0, d, (BF16,), tm, rope)
                kf, kb = _proj(xb, w, d, d, (F32, BF16), tm, rope)
                vf, vb = _proj(xb, w, 2 * d, d, (F32, BF16), tm)
                (iqb,) = _proj(xb, w, 3 * d, n_ih * idim, (BF16,), tm, rope_idx[n] + (idim // 2, rope_blocks[n]))
                o2 = 3 * d + n_ih * idim
                w_ik, w_iw = w[:, o2:o2 + idim], w[:, o2 + idim:]
                w_tail = jnp.concatenate([w_ik, w_iw, jnp.zeros((d, LANES - idim - n_ih), F32)], axis=1)
                (tail,) = _proj(xb, w_tail, 0, LANES, (F32,), tm, rope_tail[n] + (idim // 2, rope_blocks[n]))
                outs["ak"][n].append(shaped(kf, n, (n_heads, hd)))
                outs["av"][n].append(shaped(vf, n, (n_heads, hd)))
                outs["ai"][n].append(shaped(tail[:, :idim], n, (idim,)))
                if n == "p":
                    w_ik2 = jnp.concatenate([w_ik] * (LANES // idim), axis=1)
                    (ik2b,) = _proj(xb, w_ik2, 0, LANES, (BF16,), tm, rope_idx[n] + (idim // 2, rope_blocks[n]))
                    sel = _select_prompt(iqb, ik2b, tail, batch, seq, n_ih, idim, k_top[n])
                    y = _dsa_prompt(qb, kb, vb, sel, batch, seq, n_heads, hd)
                else:
                    qm = iqb.reshape(db, ds, n_ih, idim).transpose(0, 2, 1, 3).reshape(db, n_ih * ds, idim)
                    wcol = tail[:, idim:idim + n_ih].reshape(db, ds, n_ih).transpose(0, 2, 1).reshape(db, n_ih * ds, 1)
                    wcol = wcol * ((idim ** -0.5) * (n_ih ** -0.5))
                    iknew = _pad_rows(tail[:, :idim].reshape(db, ds, idim), page)
                    sel = _select_sample(page_table, qm, wcol, iknew, cache_dsa_idx_k[j], n_pages, k_top[n])
                    y = _paged_attention(
                        "dsa", page_table, qb.reshape(db, ds, d),
                        _pad_rows(kf.reshape(db, ds, d), page), _pad_rows(vf.reshape(db, ds, d), page),
                        cache_dsa_k[j].reshape(n_pool, page, d), cache_dsa_v[j].reshape(n_pool, page, d),
                        n_heads, n_pages, sel=sel).reshape(ms, d)
                w_out = w_d_out[j]
            x[n] = _mm_res(y, w_out, x[n], tm)
            fb = _rmsnorm(x[n], g_ffn[i], BF16, tm)
            hist = None if n == "p" else _expand_history(state_ffn_conv[i])
            act, st = _ffn_up(fb, w_ffn_up[i], conv_ffn[i], hist, tm, slen)
            outs["fc"][n].append(_last_two(st, nseq[n]))
            x[n] = _mm_res(act, w_ffn_down[i], x[n], tm)

    y_prompt = _rmsnorm(x["p"], g_final, F32, tm_p).reshape(batch, seq, d)
    y_sample = _rmsnorm(x["s"], g_final, F32, ms).reshape(db, ds, d)
    res = [y_prompt, y_sample]
    for name in ("conv", "sbk", "sbv", "dk", "dv", "ak", "av", "ai", "fc"):
        for n in ("p", "s"):
            res.append(jnp.stack(outs[name][n]))
    return tuple(res)
```

```python
import functools
import math

import jax
import jax.numpy as jnp
from jax import lax
from jax.experimental import pallas as pl
from jax.experimental.pallas import tpu as pltpu

NORM_EPS = 1e-6
ROPE_THETA = 10000.0
TOPK_MAX = 256
CONV_W = 3
LANES = 128
SUBLANES = 8
MXU_WIDTH = 256
FLASH_BLOCK = 512
SB_SKIP_BELOW = -104.0
VMEM_LIMIT_BYTES = 56 * 1024 * 1024
NEG_BIG = -1e30
INT_MIN = -2 ** 31

F32 = jnp.float32
BF16 = jnp.bfloat16


def _params(*sem):
    return pltpu.CompilerParams(dimension_semantics=sem, vmem_limit_bytes=VMEM_LIMIT_BYTES)


def _dot(a, b):
    return jnp.dot(a, b, preferred_element_type=F32)


def _dot_t(a, b):
    return lax.dot_general(a, b, (((1,), (1,)), ((), ())), preferred_element_type=F32)


def _col_chunk(tn):
    return MXU_WIDTH if tn % MXU_WIDTH == 0 else LANES


def _rmsnorm_kernel(x_ref, g_ref, o_ref):
    x = x_ref[...]
    ms = jnp.mean(x * x, axis=-1, keepdims=True)
    o_ref[...] = ((x * lax.rsqrt(ms + NORM_EPS)) * g_ref[...]).astype(o_ref.dtype)


def _rmsnorm(x, g, out_dtype, tm):
    m, d = x.shape
    return pl.pallas_call(
        _rmsnorm_kernel,
        grid=(m // tm,),
        in_specs=[pl.BlockSpec((tm, d), lambda i: (i, 0)), pl.BlockSpec((1, d), lambda i: (0, 0))],
        out_specs=pl.BlockSpec((tm, d), lambda i: (i, 0)),
        out_shape=jax.ShapeDtypeStruct((m, d), out_dtype),
        compiler_params=_params("arbitrary"),
    )(x, g.reshape(1, d))


def _rope_group(y, c, s, half):
    if 2 * half == LANES:
        rot = pltpu.roll(y, half, 1)
    else:
        lane = lax.broadcasted_iota(jnp.int32, y.shape, 1)
        rot = jnp.where(lane % (2 * half) < half, pltpu.roll(y, LANES - half, 1), pltpu.roll(y, half, 1))
    return y * c + rot * s


def _proj_kernel(*refs, half, n_out):
    if half:
        x_ref, w_ref, c_ref, s_ref = refs[:4]
        outs = refs[4:4 + n_out]
    else:
        x_ref, w_ref = refs[:2]
        outs = refs[2:2 + n_out]
    wb_ref = refs[-1]

    @pl.when(pl.program_id(1) == 0)
    def _():
        wb_ref[...] = w_ref[...].astype(BF16)

    x = x_ref[...]
    tn = wb_ref.shape[1]
    ch = _col_chunk(tn)
    for c0 in range(0, tn, ch):
        y = _dot(x, wb_ref[:, c0:c0 + ch])
        if half:
            c = c_ref[...]
            s = s_ref[...]
            for g in range(ch // LANES):
                sl = slice(c0 + g * LANES, c0 + (g + 1) * LANES)
                yg = _rope_group(y[:, g * LANES:(g + 1) * LANES], c, s, half)
                for o in outs:
                    o[:, sl] = yg.astype(o.dtype)
        else:
            for o in outs:
                o[:, c0:c0 + ch] = y.astype(o.dtype)


def _proj(xb, w, col0, ncols, out_dtypes, tm, rope=None):
    m, d = xb.shape
    tn = min(1024, ncols)
    assert ncols % tn == 0 and col0 % tn == 0 and m % tm == 0
    off = col0 // tn
    in_specs = [pl.BlockSpec((tm, d), lambda j, i: (i, 0)),
                pl.BlockSpec((d, tn), lambda j, i: (0, j + off))]
    args = [xb, w]
    half = 0
    if rope is not None:
        ctab, stab, half, tblocks = rope
        in_specs += [pl.BlockSpec((tm, LANES), lambda j, i: (i % tblocks, 0)),
                     pl.BlockSpec((tm, LANES), lambda j, i: (i % tblocks, 0))]
        args += [ctab, stab]
    outs = pl.pallas_call(
        functools.partial(_proj_kernel, half=half, n_out=len(out_dtypes)),
        grid=(ncols // tn, m // tm),
        in_specs=in_specs,
        out_specs=[pl.BlockSpec((tm, tn), lambda j, i: (i, j)) for _ in out_dtypes],
        out_shape=[jax.ShapeDtypeStruct((m, ncols), dt) for dt in out_dtypes],
        scratch_shapes=[pltpu.VMEM((d, tn), BF16)],
        compiler_params=_params("arbitrary", "arbitrary"),
    )(*args)
    return outs


def _mm_res_kernel(a_ref, w_ref, r_ref, o_ref, wb_ref):
    @pl.when(pl.program_id(1) == 0)
    def _():
        wb_ref[...] = w_ref[...].astype(BF16)

    o_ref[...] = r_ref[...] + _dot(a_ref[...], wb_ref[...])


def _mm_res(a, w, res, tm):
    m, k = a.shape
    n = w.shape[1]
    tn = min(512, n)
    return pl.pallas_call(
        _mm_res_kernel,
        grid=(n // tn, m // tm),
        in_specs=[pl.BlockSpec((tm, k), lambda j, i: (i, 0)),
                  pl.BlockSpec((k, tn), lambda j, i: (0, j)),
                  pl.BlockSpec((tm, tn), lambda j, i: (i, j))],
        out_specs=pl.BlockSpec((tm, tn), lambda j, i: (i, j)),
        out_shape=jax.ShapeDtypeStruct((m, n), F32),
        scratch_shapes=[pltpu.VMEM((k, tn), BF16)],
        compiler_params=_params("arbitrary", "arbitrary"),
    )(a, w, res)


def _shifted(u, prev, period):
    row = lax.broadcasted_iota(jnp.int32, u.shape, 0) % period
    u1 = jnp.where(row >= 1, pltpu.roll(u, 1, 0), pltpu.roll(prev, 1, 0))
    u2 = jnp.where(row >= 2, pltpu.roll(u, 2, 0), pltpu.roll(prev, 2, 0))
    return u1, u2


def _dwconv(u, u1, u2, cw):
    return (cw[0:1, :] * u2 + cw[1:2, :] * u1) + cw[2:3, :] * u


def _history(prev_ref, hist_ref, tiles_per_seq):
    if hist_ref is not None:
        return hist_ref[...]

    @pl.when(pl.program_id(1) % tiles_per_seq == 0)
    def _():
        prev_ref[...] = jnp.zeros_like(prev_ref)

    return prev_ref[...]


def _convmix_kernel(*refs, short, tiles_per_seq, period):
    if short:
        x_ref, wb_ref, wc_ref, wh_ref, cw_ref, hist_ref, y_ref, st_ref, wbb, wcb, whb, prev_ref = refs
    else:
        x_ref, wb_ref, wc_ref, wh_ref, cw_ref, y_ref, st_ref, wbb, wcb, whb, prev_ref = refs
        hist_ref = None

    @pl.when(pl.program_id(1) == 0)
    def _():
        wbb[...] = wb_ref[...].astype(BF16)
        wcb[...] = wc_ref[...].astype(BF16)
        whb[...] = wh_ref[...].astype(BF16)

    x = x_ref[...]
    prev_all = _history(prev_ref, hist_ref, tiles_per_seq)
    tn = wbb.shape[1]
    ch = _col_chunk(tn)
    for c0 in range(0, tn, ch):
        sl = slice(c0, c0 + ch)
        u = _dot(x, wcb[:, sl]) * _dot(x, whb[:, sl])
        u1, u2 = _shifted(u, prev_all[:, sl], period)
        conv = _dwconv(u, u1, u2, cw_ref[:, sl])
        y_ref[:, sl] = (_dot(x, wbb[:, sl]) * conv).astype(y_ref.dtype)
        if short:
            st_ref[:, sl] = u
        else:
            prev_ref[:, sl] = u
            st_ref[:, sl] = u[u.shape[0] - SUBLANES:, :]


def _state_specs(short, tm, tn, tiles_per_seq, col_off=0):
    if short:
        return pl.BlockSpec((tm, tn), lambda j, i: (0, j + col_off))
    return pl.BlockSpec((SUBLANES, tn), lambda j, i: (i // tiles_per_seq, j + col_off))


def _convmix(xb, w_in, conv_w, hist_x, tm, seq_len):
    m, d = xb.shape
    tn = min(512, d)
    nj = d // tn
    short = hist_x is not None
    if short:
        assert seq_len == SUBLANES and tm == m
        tiles_per_seq, period, nseq = 1, seq_len, m // seq_len
    else:
        assert seq_len % tm == 0
        tiles_per_seq, period, nseq = seq_len // tm, tm, m // seq_len
    in_specs = [pl.BlockSpec((tm, d), lambda j, i: (i, 0)),
                pl.BlockSpec((d, tn), lambda j, i: (0, j)),
                pl.BlockSpec((d, tn), lambda j, i: (0, j + nj)),
                pl.BlockSpec((d, tn), lambda j, i: (0, j + 2 * nj)),
                pl.BlockSpec((CONV_W, tn), lambda j, i: (0, j))]
    args = [xb, w_in, w_in, w_in, conv_w]
    if short:
        in_specs.append(pl.BlockSpec((tm, tn), lambda j, i: (0, j)))
        args.append(hist_x)
    y, st = pl.pallas_call(
        functools.partial(_convmix_kernel, short=short, tiles_per_seq=tiles_per_seq, period=period),
        grid=(nj, m // tm),
        in_specs=in_specs,
        out_specs=[pl.BlockSpec((tm, tn), lambda j, i: (i, j)), _state_specs(short, tm, tn, tiles_per_seq)],
        out_shape=[jax.ShapeDtypeStruct((m, d), BF16), jax.ShapeDtypeStruct((nseq * SUBLANES, d), F32)],
        scratch_shapes=[pltpu.VMEM((d, tn), BF16)] * 3 + [pltpu.VMEM((tm, tn), F32)],
        compiler_params=_params("arbitrary", "arbitrary"),
    )(*args)
    return y, st


def _ffn_up_kernel(*refs, short, tiles_per_seq, period):
    if short:
        (x_ref, wg_ref, wu_ref, cg_ref, cu_ref, hg_ref, hu_ref,
         a_ref, sg_ref, su_ref, wgb, wub, pg_ref, pu_ref) = refs
    else:
        x_ref, wg_ref, wu_ref, cg_ref, cu_ref, a_ref, sg_ref, su_ref, wgb, wub, pg_ref, pu_ref = refs
        hg_ref = hu_ref = None

    @pl.when(pl.program_id(1) == 0)
    def _():
        wgb[...] = wg_ref[...].astype(BF16)
        wub[...] = wu_ref[...].astype(BF16)

    x = x_ref[...]
    prevs = (_history(pg_ref, hg_ref, tiles_per_seq), _history(pu_ref, hu_ref, tiles_per_seq))
    tn = wgb.shape[1]
    ch = _col_chunk(tn)
    for c0 in range(0, tn, ch):
        sl = slice(c0, c0 + ch)
        halves = []
        for w_b, c_ref, prev, p_ref, s_ref in ((wgb, cg_ref, prevs[0], pg_ref, sg_ref),
                                               (wub, cu_ref, prevs[1], pu_ref, su_ref)):
            up = _dot(x, w_b[:, sl])
            u1, u2 = _shifted(up, prev[:, sl], period)
            halves.append(_dwconv(up, u1, u2, c_ref[:, sl]))
            if short:
                s_ref[:, sl] = up
            else:
                p_ref[:, sl] = up
                s_ref[:, sl] = up[up.shape[0] - SUBLANES:, :]
        g, u = halves
        a_ref[:, sl] = ((g * jax.nn.sigmoid(g)) * u).astype(a_ref.dtype)


def _ffn_up(xb, w_up, conv_w, hist_x, tm, seq_len):
    m, d = xb.shape
    d_ff = w_up.shape[1] // 2
    tn = 512 if d_ff % 512 == 0 else LANES
    nj = d_ff // tn
    short = hist_x is not None
    if short:
        assert seq_len == SUBLANES and tm == m
        tiles_per_seq, period, nseq = 1, seq_len, m // seq_len
    else:
        assert seq_len % tm == 0
        tiles_per_seq, period, nseq = seq_len // tm, tm, m // seq_len
    in_specs = [pl.BlockSpec((tm, d), lambda j, i: (i, 0)),
                pl.BlockSpec((d, tn), lambda j, i: (0, j)),
                pl.BlockSpec((d, tn), lambda j, i: (0, j + nj)),
                pl.BlockSpec((CONV_W, tn), lambda j, i: (0, j)),
                pl.BlockSpec((CONV_W, tn), lambda j, i: (0, j + nj))]
    args = [xb, w_up, w_up, conv_w, conv_w]
    if short:
        in_specs += [pl.BlockSpec((tm, tn), lambda j, i: (0, j)), pl.BlockSpec((tm, tn), lambda j, i: (0, j + nj))]
        args += [hist_x, hist_x]
    act, sg, su = pl.pallas_call(
        functools.partial(_ffn_up_kernel, short=short, tiles_per_seq=tiles_per_seq, period=period),
        grid=(nj, m // tm),
        in_specs=in_specs,
        out_specs=[pl.BlockSpec((tm, tn), lambda j, i: (i, j)),
                   _state_specs(short, tm, tn, tiles_per_seq),
                   _state_specs(short, tm, tn, tiles_per_seq)],
        out_shape=[jax.ShapeDtypeStruct((m, d_ff), BF16),
                   jax.ShapeDtypeStruct((nseq * SUBLANES, d_ff), F32),
                   jax.ShapeDtypeStruct((nseq * SUBLANES, d_ff), F32)],
        scratch_shapes=[pltpu.VMEM((d, tn), BF16)] * 2 + [pltpu.VMEM((tm, tn), F32)] * 2,
        compiler_params=_params("arbitrary", "arbitrary"),
    )(*args)
    return act, jnp.concatenate([sg, su], axis=1)


def _log_sigmoid_pair(z):
    l1p = jnp.log(1.0 + jnp.exp(-jnp.abs(z)))
    ls = jnp.minimum(z, 0.0) - l1p
    return ls, ls - z


def _suffix_sum(x, tri):
    hi = x.astype(BF16)
    lo = (x - hi.astype(F32)).astype(BF16)
    return _dot(hi, tri) + _dot(lo, tri)


def _tri(n):
    j = lax.broadcasted_iota(jnp.int32, (n, n), 0)
    s = lax.broadcasted_iota(jnp.int32, (n, n), 1)
    return (j > s).astype(BF16)


def _sb_prompt_kernel(q_ref, k_ref, v_ref, tri_ref, o_ref, *, tq, scale):
    qi = pl.program_id(2)
    q = q_ref[...]
    tri = tri_ref[...]

    def block(ki, carry, acc, diag):
        start = pl.multiple_of(ki * tq, tq)
        k = k_ref[pl.ds(start, tq), :]
        v = v_ref[pl.ds(start, tq), :]
        z = _dot_t(q, k) * scale
        ls, l1m = _log_sigmoid_pair(z)
        if diag:
            row = lax.broadcasted_iota(jnp.int32, z.shape, 0)
            col = lax.broadcasted_iota(jnp.int32, z.shape, 1)
            valid = col < row
            l1m = jnp.where(valid, l1m, 0.0)
        after = _suffix_sum(l1m, tri)
        a = jnp.exp(ls + after + carry)
        if diag:
            a = jnp.where(valid, a, 0.0)
        acc = acc + _dot(a.astype(BF16), v)
        carry = carry + jnp.sum(l1m, axis=1, keepdims=True)
        return carry, acc

    carry0 = jnp.zeros((tq, 1), F32)
    acc0 = jnp.zeros((tq, v_ref.shape[1]), F32)
    carry, acc = block(qi, carry0, acc0, True)

    def more(c):
        return (c[0] < qi) & (jnp.max(c[1]) > SB_SKIP_BELOW)

    def body(c):
        carry, acc = block(qi - 1 - c[0], c[1], c[2], False)
        return c[0] + 1, carry, acc

    _, carry, acc = lax.while_loop(more, body, (jnp.int32(0), carry, acc))
    o_ref[...] = acc.astype(o_ref.dtype)


def _sb_prompt(qb, kb, vb, batch, seq, n_heads, hd):
    m, d = qb.shape
    tq = min(256, seq)
    nq = seq // tq
    return pl.pallas_call(
        functools.partial(_sb_prompt_kernel, tq=tq, scale=hd ** -0.5),
        grid=(batch, n_heads, nq),
        in_specs=[pl.BlockSpec((tq, hd), lambda b, h, q: (b * nq + q, h)),
                  pl.BlockSpec((seq, hd), lambda b, h, q: (b, h)),
                  pl.BlockSpec((seq, hd), lambda b, h, q: (b, h)),
                  pl.BlockSpec((tq, tq), lambda b, h, q: (0, 0))],
        out_specs=pl.BlockSpec((tq, hd), lambda b, h, q: (b * nq + q, h)),
        out_shape=jax.ShapeDtypeStruct((m, d), BF16),
        compiler_params=_params("arbitrary", "arbitrary", "arbitrary"),
    )(qb, kb, vb, _tri(tq))


def _online_softmax_step(s, mask, m, l, acc, v):
    if mask is not None:
        s = jnp.where(mask, s, NEG_BIG)
    m_new = jnp.maximum(m, jnp.max(s, axis=1, keepdims=True))
    p = jnp.exp(s - m_new)
    if mask is not None:
        p = jnp.where(mask, p, 0.0)
    alpha = jnp.exp(m - m_new)
    l = alpha * l + jnp.sum(p, axis=1, keepdims=True)
    acc = alpha * acc + _dot(p.astype(BF16), v)
    return m_new, l, acc


def _lambda_full(lam_ref, lambda_init):
    lf = lam_ref[...]
    s01 = jnp.sum(lf[0:1, :] * lf[1:2, :], axis=1, keepdims=True)
    s23 = jnp.sum(lf[2:3, :] * lf[3:4, :], axis=1, keepdims=True)
    return jnp.exp(s01) - jnp.exp(s23) + lambda_init


def _sub_norm(o, g, lambda_init):
    ms = jnp.mean(o * o, axis=-1, keepdims=True)
    return ((o * lax.rsqrt(ms + NORM_EPS)) * g) * (1.0 - lambda_init)


def _diff_prompt_kernel(q_ref, k_ref, v_ref, lam_ref, g_ref, o_ref, *, tq, hd, scale, lambda_init):
    qi = pl.program_id(2)
    dv = v_ref.shape[1]

    def block(ki, state, diag):
        start = pl.multiple_of(ki * tq, tq)
        v = v_ref[pl.ds(start, tq), :]
        mask = None
        if diag:
            row = lax.broadcasted_iota(jnp.int32, (tq, tq), 0)
            col = lax.broadcasted_iota(jnp.int32, (tq, tq), 1)
            mask = col <= row
        new = []
        for i in range(2):
            q = q_ref[:, i * hd:(i + 1) * hd]
            k = k_ref[pl.ds(start, tq), i * hd:(i + 1) * hd]
            s = _dot_t(q, k) * scale
            new.append(_online_softmax_step(s, mask, *state[i], v))
        return tuple(new)

    init = tuple((jnp.full((tq, 1), NEG_BIG, F32), jnp.zeros((tq, 1), F32), jnp.zeros((tq, dv), F32))
                 for _ in range(2))
    state = lax.fori_loop(0, qi, lambda ki, st: block(ki, st, False), init)
    state = block(qi, state, True)
    lam = _lambda_full(lam_ref, lambda_init)
    (_, l0, a0), (_, l1, a1) = state
    o = a0 / l0 - lam * (a1 / l1)
    o_ref[...] = _sub_norm(o, g_ref[...], lambda_init).astype(o_ref.dtype)


def _diff_prompt(qb, kb, vb, lam, g_sub, batch, seq, n_dh, hd, lambda_init):
    m, d = qb.shape
    tq = min(FLASH_BLOCK, seq)
    nq = seq // tq
    w = 2 * hd
    return pl.pallas_call(
        functools.partial(_diff_prompt_kernel, tq=tq, hd=hd, scale=hd ** -0.5, lambda_init=lambda_init),
        grid=(batch, n_dh, nq),
        in_specs=[pl.BlockSpec((tq, w), lambda b, h, q: (b * nq + q, h)),
                  pl.BlockSpec((seq, w), lambda b, h, q: (b, h)),
                  pl.BlockSpec((seq, w), lambda b, h, q: (b, h)),
                  pl.BlockSpec(lam.shape, lambda b, h, q: (0, 0)),
                  pl.BlockSpec((1, w), lambda b, h, q: (0, 0))],
        out_specs=pl.BlockSpec((tq, w), lambda b, h, q: (b * nq + q, h)),
        out_shape=jax.ShapeDtypeStruct((m, d), BF16),
        compiler_params=_params("arbitrary", "arbitrary", "arbitrary"),
    )(qb, kb, vb, lam, g_sub.reshape(1, w))


def _dsa_prompt_kernel(q_ref, k_ref, v_ref, sel_ref, o_ref, *, tq, scale):
    qi = pl.program_id(2)
    q = q_ref[...]
    dv = v_ref.shape[1]

    def block(ki, state):
        start = pl.multiple_of(ki * tq, tq)
        k = k_ref[pl.ds(start, tq), :]
        v = v_ref[pl.ds(start, tq), :]
        mask = sel_ref[:, pl.ds(start, tq)].astype(F32) > 0.5
        s = _dot_t(q, k) * scale
        return _online_softmax_step(s, mask, *state, v)

    init = (jnp.full((tq, 1), NEG_BIG, F32), jnp.zeros((tq, 1), F32), jnp.zeros((tq, dv), F32))
    _, l, acc = lax.fori_loop(0, qi + 1, block, init)
    o_ref[...] = (acc / l).astype(o_ref.dtype)


def _dsa_prompt(qb, kb, vb, sel, batch, seq, n_heads, hd):
    m, d = qb.shape
    tq = min(FLASH_BLOCK, seq)
    nq = seq // tq
    return pl.pallas_call(
        functools.partial(_dsa_prompt_kernel, tq=tq, scale=hd ** -0.5),
        grid=(batch, n_heads, nq),
        in_specs=[pl.BlockSpec((tq, hd), lambda b, h, q: (b * nq + q, h)),
                  pl.BlockSpec((seq, hd), lambda b, h, q: (b, h)),
                  pl.BlockSpec((seq, hd), lambda b, h, q: (b, h)),
                  pl.BlockSpec((tq, seq), lambda b, h, q: (b * nq + q, 0))],
        out_specs=pl.BlockSpec((tq, hd), lambda b, h, q: (b * nq + q, h)),
        out_shape=jax.ShapeDtypeStruct((m, d), BF16),
        compiler_params=_params("arbitrary", "arbitrary", "arbitrary"),
    )(qb, kb, vb, sel)


def _order_key(x):
    bits = pltpu.bitcast(x + 0.0, jnp.int32)
    return jnp.where(bits >= 0, bits, bits ^ jnp.int32(0x7FFFFFFF))


def _kth_largest_key(count_ge, rows, k):
    zero = jnp.zeros((rows, 1), jnp.int32)
    t0 = jnp.where(count_ge(zero) >= k, zero, jnp.int32(INT_MIN))

    def body(it, t):
        cand = t | jnp.left_shift(jnp.int32(1), jnp.int32(30) - it)
        return jnp.where(count_ge(cand) >= k, cand, t)

    return lax.fori_loop(0, 31, body, t0)


def _store_topk_mask(o_ref, key, valid, k, t):
    ge = valid & (key >= t)
    n_ge = jnp.sum(ge.astype(jnp.int32), axis=1, keepdims=True)
    has_ties = jnp.max(n_ge) > k

    @pl.when(jnp.logical_not(has_ties))
    def _():
        o_ref[...] = jnp.where(ge, 1.0, 0.0).astype(o_ref.dtype)

    @pl.when(has_ties)
    def _():
        gt = key > t
        eq = key == t
        need = k - jnp.sum(gt.astype(jnp.int32), axis=1, keepdims=True)
        col = lax.broadcasted_iota(jnp.int32, key.shape, 1)
        nbits = int(key.shape[1]).bit_length()

        def body(it, j):
            cand = j | jnp.left_shift(jnp.int32(1), jnp.int32(nbits - 1) - it)
            c = jnp.sum((eq & (col < cand)).astype(jnp.int32), axis=1, keepdims=True)
            return jnp.where(c < need, cand, j)

        j = lax.fori_loop(0, nbits, body, jnp.zeros_like(t))
        sel = valid & (gt | (eq & (col <= j)))
        o_ref[...] = jnp.where(sel, 1.0, 0.0).astype(o_ref.dtype)


def _select_prompt_kernel(iq_ref, ik_ref, iw_ref, o_ref, key_ref, qh_ref, *, tq, cw, n_ih, idim, k_top, wscale):
    qi = pl.program_id(1)
    seq = key_ref.shape[1]
    n_need = (qi * tq) // cw + 1
    iw = iw_ref[...] * wscale
    lane = lax.broadcasted_iota(jnp.int32, (tq, LANES), 1)
    per_group = LANES // idim
    for h in range(n_ih):
        g, r = divmod(h, per_group)
        qg = iq_ref[:, g * LANES:(g + 1) * LANES].astype(F32)
        qh_ref[h] = jnp.where((lane >= r * idim) & (lane < (r + 1) * idim), qg, 0.0).astype(BF16)

    def chunk_at(c):
        return pl.ds(pl.multiple_of(c * cw, cw), cw)

    def score_chunk(c, carry):
        ikc = ik_ref[chunk_at(c), :]
        acc = None
        for h in range(n_ih):
            s = jnp.maximum(_dot_t(qh_ref[h], ikc), 0.0) * iw[:, idim + h:idim + h + 1]
            acc = s if acc is None else acc + s
        row = qi * tq + lax.broadcasted_iota(jnp.int32, (tq, cw), 0)
        col = c * cw + lax.broadcasted_iota(jnp.int32, (tq, cw), 1)
        key_ref[:, chunk_at(c)] = _order_key(jnp.where(col <= row, acc, -jnp.inf))
        return carry

    def fill_chunk(c, carry):
        key_ref[:, chunk_at(c)] = jnp.full((tq, cw), INT_MIN, jnp.int32)
        return carry

    lax.fori_loop(0, n_need, score_chunk, 0)
    lax.fori_loop(n_need, seq // cw, fill_chunk, 0)

    def count_ge(t):
        def body(c, acc):
            m = (key_ref[:, chunk_at(c)] >= t).astype(jnp.int32)
            for v in range(cw // LANES):
                acc = acc + m[:, v * LANES:(v + 1) * LANES]
            return acc

        acc = lax.fori_loop(0, n_need, body, jnp.zeros((tq, LANES), jnp.int32))
        return jnp.sum(acc, axis=1, keepdims=True)

    t = _kth_largest_key(count_ge, tq, k_top)
    row = qi * tq + lax.broadcasted_iota(jnp.int32, (tq, seq), 0)
    col = lax.broadcasted_iota(jnp.int32, (tq, seq), 1)
    _store_topk_mask(o_ref, key_ref[...], col <= row, k_top, t)


def _select_prompt(iqb, ik2b, tail, batch, seq, n_ih, idim, k_top):
    m = iqb.shape[0]
    tq = min(128, seq)
    cw = min(512, seq)
    nq = seq // tq
    assert LANES % idim == 0 and cw % tq == 0 and seq % cw == 0
    return pl.pallas_call(
        functools.partial(_select_prompt_kernel, tq=tq, cw=cw, n_ih=n_ih, idim=idim, k_top=k_top,
                          wscale=(idim ** -0.5) * (n_ih ** -0.5)),
        grid=(batch, nq),
        in_specs=[pl.BlockSpec((tq, n_ih * idim), lambda b, q: (b * nq + q, 0)),
                  pl.BlockSpec((seq, LANES), lambda b, q: (b, 0)),
                  pl.BlockSpec((tq, LANES), lambda b, q: (b * nq + q, 0))],
        out_specs=pl.BlockSpec((tq, seq), lambda b, q: (b * nq + q, 0)),
        out_shape=jax.ShapeDtypeStruct((m, seq), BF16),
        scratch_shapes=[pltpu.VMEM((tq, seq), jnp.int32), pltpu.VMEM((n_ih, tq, LANES), BF16)],
        compiler_params=_params("arbitrary", "arbitrary"),
    )(iqb, ik2b, tail)


def _sel_sample_kernel(pt_ref, qm_ref, w_ref, iknew_ref, *refs, pages_per_step, n_pages, ds, k_top):
    ik_refs = refs[:pages_per_step]
    o_ref = refs[pages_per_step]
    sc_ref = refs[pages_per_step + 1]
    s_id = pl.program_id(1)
    qm = qm_ref[...]
    w = w_ref[...]
    page = ik_refs[0].shape[0]

    def scores(ik):
        s = jnp.maximum(_dot_t(qm, ik.astype(BF16)), 0.0) * w
        tot = s[0:ds, :]
        for h in range(1, s.shape[0] // ds):
            tot = tot + s[h * ds:(h + 1) * ds, :]
        return tot

    for c in range(pages_per_step):
        pg = s_id * pages_per_step + c
        sc_ref[:, pl.ds(pl.multiple_of(pg * page, page), page)] = scores(ik_refs[c][...])

    @pl.when(s_id == pl.num_programs(1) - 1)
    def _():
        t = lax.broadcasted_iota(jnp.int32, (ds, page), 0)
        j = lax.broadcasted_iota(jnp.int32, (ds, page), 1)
        sc_ref[:, n_pages * page:] = jnp.where(j <= t, scores(iknew_ref[...]), -jnp.inf)
        col = lax.broadcasted_iota(jnp.int32, sc_ref.shape, 1)
        row = lax.broadcasted_iota(jnp.int32, sc_ref.shape, 0)
        key = _order_key(sc_ref[...])

        def count_ge(thr):
            return jnp.sum((key >= thr).astype(jnp.int32), axis=1, keepdims=True)

        thr = _kth_largest_key(count_ge, ds, k_top)
        _store_topk_mask(o_ref, key, col <= n_pages * page + row, k_top, thr)


def _select_sample(page_table, qm, wcol, iknew, pool, n_pages, k_top):
    db, rows, idim = qm.shape
    page = pool.shape[1]
    ds = SUBLANES
    pps = 8 if n_pages % 8 == 0 else (4 if n_pages % 4 == 0 else 1)
    n_steps = n_pages // pps
    nk = (n_pages + 1) * page

    def pool_spec(c):
        return pl.BlockSpec((None, page, idim), lambda b, s, pt: (pt[b, s * pps + c], 0, 0))

    grid_spec = pltpu.PrefetchScalarGridSpec(
        num_scalar_prefetch=1,
        grid=(db, n_steps),
        in_specs=[pl.BlockSpec((None, rows, idim), lambda b, s, pt: (b, 0, 0)),
                  pl.BlockSpec((None, rows, 1), lambda b, s, pt: (b, 0, 0)),
                  pl.BlockSpec((None, page, idim), lambda b, s, pt: (b, 0, 0))]
                 + [pool_spec(c) for c in range(pps)],
        out_specs=pl.BlockSpec((None, ds, nk), lambda b, s, pt: (b, 0, 0)),
        scratch_shapes=[pltpu.VMEM((ds, nk), F32)],
    )
    return pl.pallas_call(
        functools.partial(_sel_sample_kernel, pages_per_step=pps, n_pages=n_pages, ds=ds, k_top=k_top),
        grid_spec=grid_spec,
        out_shape=jax.ShapeDtypeStruct((db, ds, nk), BF16),
        compiler_params=_params("arbitrary", "arbitrary"),
    )(page_table, qm, wcol, iknew, *([pool] * pps))


def _class_reduce(x, n_cls, op):
    r = x[:, 0:LANES]
    for v in range(1, x.shape[1] // LANES):
        r = op(r, x[:, v * LANES:(v + 1) * LANES])
    sh = LANES // 2
    while sh >= n_cls:
        r = op(r, pltpu.roll(r, sh, 1))
        sh //= 2
    return r


def _tile_lanes(r, ncol):
    return jnp.concatenate([r] * (ncol // LANES), axis=1)


def _class_column(r, n_cls):
    return jnp.concatenate([r[:, c:c + 1] for c in range(n_cls)], axis=0)


def _suffix_by_class(x, n_cls):
    lane = lax.broadcasted_iota(jnp.int32, (x.shape[0], LANES), 1)
    nv = x.shape[1] // LANES
    excl_in, tots = [], []
    for v in range(nv):
        xv = x[:, v * LANES:(v + 1) * LANES]
        inc = xv
        sh = n_cls
        while sh < LANES:
            inc = inc + jnp.where(lane < LANES - sh, pltpu.roll(inc, LANES - sh, 1), 0.0)
            sh *= 2
        tot = xv
        sh = LANES // 2
        while sh >= n_cls:
            tot = tot + pltpu.roll(tot, sh, 1)
            sh //= 2
        excl_in.append(inc - xv)
        tots.append(tot)
    later = jnp.zeros((x.shape[0], LANES), F32)
    out = [None] * nv
    for v in reversed(range(nv)):
        out[v] = excl_in[v] + later
        later = later + tots[v]
    return jnp.concatenate(out, axis=1), later


def _paged_kernel(pt_ref, *refs, kind, pages_per_step, n_cls, n_grp, scale, lambda_init):
    n_in = 3 + (2 if kind == "dsa" else 0) + 2 * pages_per_step + (2 if kind == "diff" else 0)
    ins, o_ref, (m_ref, l_ref, acc_ref) = refs[:n_in], refs[n_in], refs[n_in + 1:]
    q_ref, knew_ref, vnew_ref = ins[:3]
    pos = 3
    if kind == "dsa":
        selnew_ref, sel_ref = ins[pos:pos + 2]
        pos += 2
    k_refs = ins[pos:pos + pages_per_step]
    v_refs = ins[pos + pages_per_step:pos + 2 * pages_per_step]
    pos += 2 * pages_per_step
    if kind == "diff":
        lam_ref, g_ref = ins[pos:pos + 2]
    s_id = pl.program_id(1)
    page = knew_ref.shape[0]
    ncol = page * n_cls
    col = lax.broadcasted_iota(jnp.int32, (SUBLANES, ncol), 1)
    cls = col % n_cls

    def fold(s_big):
        out = s_big[0:SUBLANES, :]
        for c in range(1, n_cls):
            out = jnp.where(cls == c, s_big[c * SUBLANES:(c + 1) * SUBLANES, :], out)
        return out

    def unfold(p):
        return jnp.concatenate([jnp.where(cls == c, p, 0.0) for c in range(n_cls)], axis=0).astype(BF16)

    def process(k_ref, v_ref, mask):
        v = v_ref[...]
        v2 = v.reshape(ncol, v.shape[-1]).astype(BF16)
        for g in range(n_grp):
            k = k_ref[...] if n_grp == 1 else k_ref[:, pl.ds(g, n_cls, stride=n_grp), :]
            k2 = k.reshape(ncol, k.shape[-1]).astype(BF16)
            s = fold(_dot_t(q_ref[g], k2) * scale)
            if kind == "sb":
                ls, l1m = _log_sigmoid_pair(s)
                if mask is not None:
                    l1m = jnp.where(mask, l1m, 0.0)
                excl, total = _suffix_by_class(l1m, n_cls)
                a = jnp.exp(ls + excl + _tile_lanes(m_ref[g], ncol))
                if mask is not None:
                    a = jnp.where(mask, a, 0.0)
                acc_ref[g] += _dot(unfold(a), v2)
                m_ref[g] += total
            else:
                if mask is not None:
                    s = jnp.where(mask, s, NEG_BIG)
                m_old = m_ref[g]
                m_new = jnp.maximum(m_old, _class_reduce(s, n_cls, jnp.maximum))
                p = jnp.exp(s - _tile_lanes(m_new, ncol))
                if mask is not None:
                    p = jnp.where(mask, p, 0.0)
                alpha = jnp.exp(m_old - m_new)
                l_ref[g] = alpha * l_ref[g] + _class_reduce(p, n_cls, jnp.add)
                m_ref[g] = m_new
                acc_ref[g] = acc_ref[g] * _class_column(alpha, n_cls) + _dot(unfold(p), v2)

    @pl.when(s_id == 0)
    def _():
        acc_ref[...] = jnp.zeros_like(acc_ref)
        l_ref[...] = jnp.zeros_like(l_ref)
        m_ref[...] = jnp.zeros_like(m_ref) if kind == "sb" else jnp.full_like(m_ref, NEG_BIG)
        t = lax.broadcasted_iota(jnp.int32, (SUBLANES, ncol), 0)
        slot = col // n_cls
        if kind == "sb":
            mask = slot < t
        elif kind == "diff":
            mask = slot <= t
        else:
            mask = selnew_ref[...].astype(F32) > 0.5
        process(knew_ref, vnew_ref, mask)

    for c in range(pages_per_step):
        mask = None
        if kind == "dsa":
            lo = (pages_per_step - 1 - c) * ncol
            mask = sel_ref[:, lo:lo + ncol].astype(F32) > 0.5
        process(k_refs[c], v_refs[c], mask)

    @pl.when(s_id == pl.num_programs(1) - 1)
    def _():
        if kind == "sb":
            o_ref[...] = acc_ref[0].astype(o_ref.dtype)
        elif kind == "dsa":
            o_ref[...] = (acc_ref[0] / _class_column(l_ref[0], n_cls)).astype(o_ref.dtype)
        else:
            lam = _lambda_full(lam_ref, lambda_init)
            a0 = acc_ref[0] / _class_column(l_ref[0], n_cls)
            a1 = acc_ref[1] / _class_column(l_ref[1], n_cls)
            o_ref[...] = _sub_norm(a0 - lam * a1, g_ref[...], lambda_init).astype(o_ref.dtype)


def _paged_attention(kind, page_table, layer, q, knew, vnew, kpool, vpool, n_pages, *,
                     sel=None, lam=None, g_sub=None, lambda_init=0.0):
    db, n_grp, rows, hd = q.shape
    n_cls = rows // SUBLANES
    page = kpool.shape[2]
    dv = vpool.shape[4]
    ncol = page * n_cls
    pps = 4 if n_pages % 4 == 0 else 1
    n_steps = n_pages // pps

    def pool_spec(pool, c):
        return pl.BlockSpec((None, None) + pool.shape[2:],
                            lambda b, s, pt: (layer, pt[b, n_pages - 1 - (s * pps + c)], 0, 0, 0))

    in_specs = [pl.BlockSpec((None,) + q.shape[1:], lambda b, s, pt: (b, 0, 0, 0)),
                pl.BlockSpec((None,) + knew.shape[1:], lambda b, s, pt: (b, 0, 0, 0)),
                pl.BlockSpec((None,) + vnew.shape[1:], lambda b, s, pt: (b, 0, 0, 0))]
    args = [q, knew, vnew]
    if kind == "dsa":
        in_specs += [pl.BlockSpec((None, SUBLANES, ncol), lambda b, s, pt: (b, 0, n_pages)),
                     pl.BlockSpec((None, SUBLANES, pps * ncol), lambda b, s, pt: (b, 0, n_steps - 1 - s))]
        args += [sel, sel]
    in_specs += [pool_spec(kpool, c) for c in range(pps)] + [pool_spec(vpool, c) for c in range(pps)]
    args += [kpool] * pps + [vpool] * pps
    if kind == "diff":
        in_specs += [pl.BlockSpec(lam.shape, lambda b, s, pt: (0, 0)),
                     pl.BlockSpec((1, dv), lambda b, s, pt: (0, 0))]
        args += [lam, g_sub.reshape(1, dv)]
    grid_spec = pltpu.PrefetchScalarGridSpec(
        num_scalar_prefetch=1,
        grid=(db, n_steps),
        in_specs=in_specs,
        out_specs=pl.BlockSpec((None, rows, dv), lambda b, s, pt: (b, 0, 0)),
        scratch_shapes=[pltpu.VMEM((n_grp, SUBLANES, LANES), F32), pltpu.VMEM((n_grp, SUBLANES, LANES), F32),
                        pltpu.VMEM((n_grp, rows, dv), F32)],
    )
    return pl.pallas_call(
        functools.partial(_paged_kernel, kind=kind, pages_per_step=pps, n_cls=n_cls, n_grp=n_grp,
                          scale=hd ** -0.5, lambda_init=lambda_init),
        grid_spec=grid_spec,
        out_shape=jax.ShapeDtypeStruct((db, rows, dv), BF16),
        compiler_params=_params("arbitrary", "arbitrary"),
    )(page_table, *args)


def _rope_tables(pos, half, ident_from=None):
    inv = ROPE_THETA ** (-jnp.arange(half, dtype=F32) / half)
    ang = pos.astype(F32)[:, None] * inv
    cos, sin = jnp.cos(ang), jnp.sin(ang)
    reps = LANES // (2 * half)
    c = jnp.tile(jnp.concatenate([cos, cos], axis=1), (1, reps))
    s = jnp.tile(jnp.concatenate([-sin, sin], axis=1), (1, reps))
    if ident_from is not None:
        lane = jnp.arange(LANES)[None, :]
        c = jnp.where(lane < ident_from, c, 1.0)
        s = jnp.where(lane < ident_from, s, 0.0)
    return c, s


def _expand_history(hist):
    nseq, _, c = hist.shape
    z = jnp.zeros((nseq, SUBLANES - 2, c), hist.dtype)
    blocks = jnp.concatenate([z, hist], axis=1).reshape(nseq * SUBLANES, c)
    return jnp.roll(blocks, -SUBLANES, axis=0)


def _last_two(st, nseq):
    return st.reshape(nseq, SUBLANES, st.shape[1])[:, SUBLANES - 2:, :]


def _pad_rows(x, rows):
    return jnp.pad(x, ((0, 0), (0, rows - x.shape[1])) + ((0, 0),) * (x.ndim - 2))


def _rows_by_class(x, db, ds, n_grp, n_cls):
    hd = x.shape[1] // (n_cls * n_grp)
    return x.reshape(db, ds, n_cls, n_grp, hd).transpose(0, 3, 2, 1, 4).reshape(db, n_grp, n_cls * ds, hd)


def _rows_by_query(y, db, ds):
    n_cls, dv = y.shape[1] // ds, y.shape[2]
    return y.reshape(db, n_cls, ds, dv).transpose(0, 2, 1, 3).reshape(db * ds, n_cls * dv)


def kernel(x_prompt, x_sample, state_conv_mix, cache_sb_k, cache_sb_v, cache_diff_k, cache_diff_v, cache_dsa_k, cache_dsa_v, cache_dsa_idx_k, state_ffn_conv, page_table, g_mix, g_ffn, g_final, w_a_in, conv_a, w_a_out, w_b_qkv, w_b_out, w_c_qkv, lam_c, g_c_subln, w_c_out, w_d_in, w_d_out, w_ffn_up, conv_ffn, w_ffn_down):
    batch, seq, d = x_prompt.shape
    db, ds, _ = x_sample.shape
    depth = g_mix.shape[0]
    n_pool, page = cache_sb_k.shape[1], cache_sb_k.shape[2]
    n_heads, hd = cache_sb_k.shape[3], cache_sb_k.shape[4]
    n_dh = cache_diff_k.shape[3]
    idim = cache_dsa_idx_k.shape[3]
    n_pages = page_table.shape[1]
    past = n_pages * page
    n_ih = (w_d_in.shape[2] - 3 * d - idim) // (idim + 1)
    assert ds == SUBLANES and hd == LANES and cache_diff_k.shape[5] == hd and n_dh * 2 == n_heads
    assert (n_ih * idim) % LANES == 0 and idim + n_ih <= LANES

    mp, ms = batch * seq, db * ds
    tm_p = min(512, seq)
    streams = (("p", tm_p, seq), ("s", ms, ds))
    x = {"p": x_prompt.reshape(mp, d), "s": x_sample.reshape(ms, d)}
    pos = {"p": jnp.arange(seq), "s": past + jnp.arange(ds)}
    pos_rows = {"p": pos["p"], "s": jnp.tile(pos["s"], db)}
    rope_blocks = {"p": seq // tm_p, "s": 1}
    rope_hd = {n: _rope_tables(pos_rows[n], hd // 2) for n in x}
    rope_idx = {n: _rope_tables(pos_rows[n], idim // 2) for n in x}
    rope_tail = {n: _rope_tables(pos_rows[n], idim // 2, ident_from=idim) for n in x}
    k_top = {"p": min(TOPK_MAX, seq // 4), "s": min(TOPK_MAX, (past + ds) // 4)}
    nseq = {"p": batch, "s": db}

    outs = {name: {"p": [], "s": []} for name in
            ("conv", "sbk", "sbv", "dk", "dv", "ak", "av", "ai", "fc")}

    def shaped(a, n, tail):
        return a.reshape((nseq[n], seq if n == "p" else ds) + tail)

    for i in range(depth):
        kind, j = i % 4, i // 4
        for n, tm, slen in streams:
            xb = _rmsnorm(x[n], g_mix[i], BF16, tm)
            if kind == 0:
                hist = None if n == "p" else _expand_history(state_conv_mix[j])
                y, st = _convmix(xb, w_a_in[j], conv_a[j], hist, tm, slen)
                outs["conv"][n].append(_last_two(st, nseq[n]))
                w_out = w_a_out[j]
            elif kind == 1:
                w = w_b_qkv[j]
                (qb,) = _proj(xb, w, 0, d, (BF16,), tm)
                kf, kb = _proj(xb, w, d, d, (F32, BF16), tm)
                vf, vb = _proj(xb, w, 2 * d, d, (F32, BF16), tm)
                outs["sbk"][n].append(shaped(kf, n, (n_heads, hd)))
                outs["sbv"][n].append(shaped(vf, n, (n_heads, hd)))
                if n == "p":
                    y = _sb_prompt(qb, kb, vb, batch, seq, n_heads, hd)
                else:
                    y = _rows_by_query(_paged_attention(
                        "sb", page_table, j, _rows_by_class(qb, db, ds, 1, n_heads),
                        _pad_rows(kf.reshape(db, ds, n_heads, hd), page),
                        _pad_rows(vf.reshape(db, ds, n_heads, hd), page),
                        cache_sb_k, cache_sb_v, n_pages), db, ds)
                w_out = w_b_out[j]
            elif kind == 2:
                lambda_init = 0.8 - 0.6 * math.exp(-0.3 * i)
                w = w_c_qkv[j]
                rope = rope_hd[n] + (hd // 2, rope_blocks[n])
                (qb,) = _proj(xb, w, 0, d, (BF16,), tm, rope)
                kf, kb = _proj(xb, w, d, d, (F32, BF16), tm, rope)
                vf, vb = _proj(xb, w, 2 * d, d, (F32, BF16), tm)
                outs["dk"][n].append(shaped(kf, n, (n_dh, 2, hd)))
                outs["dv"][n].append(shaped(vf, n, (n_dh, 2 * hd)))
                if n == "p":
                    y = _diff_prompt(qb, kb, vb, lam_c[j], g_c_subln[j], batch, seq, n_dh, hd, lambda_init)
                else:
                    y = _rows_by_query(_paged_attention(
                        "diff", page_table, j, _rows_by_class(qb, db, ds, 2, n_dh),
                        _pad_rows(kf.reshape(db, ds, n_heads, hd), page),
                        _pad_rows(vf.reshape(db, ds, n_dh, 2 * hd), page),
                        cache_diff_k.reshape(cache_diff_k.shape[:3] + (n_heads, hd)), cache_diff_v, n_pages,
                        lam=lam_c[j], g_sub=g_c_subln[j], lambda_init=lambda_init), db, ds)
                w_out = w_c_out[j]
            else:
                w = w_d_in[j]
                rope = rope_hd[n] + (hd // 2, rope_blocks[n])
                (qb,) = _proj(xb, w, 0, d, (BF16,), tm, rope)
                kf, kb = _proj(xb, w, d, d, (F32, BF16), tm, rope)
                vf, vb = _proj(xb, w, 2 * d, d, (F32, BF16), tm)
                (iqb,) = _proj(xb, w, 3 * d, n_ih * idim, (BF16,), tm, rope_idx[n] + (idim // 2, rope_blocks[n]))
                o2 = 3 * d + n_ih * idim
                w_ik, w_iw = w[:, o2:o2 + idim], w[:, o2 + idim:]
                w_tail = jnp.concatenate([w_ik, w_iw, jnp.zeros((d, LANES - idim - n_ih), F32)], axis=1)
                (tail,) = _proj(xb, w_tail, 0, LANES, (F32,), tm, rope_tail[n] + (idim // 2, rope_blocks[n]))
                outs["ak"][n].append(shaped(kf, n, (n_heads, hd)))
                outs["av"][n].append(shaped(vf, n, (n_heads, hd)))
                outs["ai"][n].append(shaped(tail[:, :idim], n, (idim,)))
                if n == "p":
                    w_ik2 = jnp.concatenate([w_ik] * (LANES // idim), axis=1)
                    (ik2b,) = _proj(xb, w_ik2, 0, LANES, (BF16,), tm, rope_idx[n] + (idim // 2, rope_blocks[n]))
                    sel = _select_prompt(iqb, ik2b, tail, batch, seq, n_ih, idim, k_top[n])
                    y = _dsa_prompt(qb, kb, vb, sel, batch, seq, n_heads, hd)
                else:
                    qm = iqb.reshape(db, ds, n_ih, idim).transpose(0, 2, 1, 3).reshape(db, n_ih * ds, idim)
                    wcol = tail[:, idim:idim + n_ih].reshape(db, ds, n_ih).transpose(0, 2, 1).reshape(db, n_ih * ds, 1)
                    wcol = wcol * ((idim ** -0.5) * (n_ih ** -0.5))
                    iknew = _pad_rows(tail[:, :idim].reshape(db, ds, idim), page)
                    sel = _select_sample(page_table, qm, wcol, iknew, cache_dsa_idx_k[j], n_pages, k_top[n])
                    y = _rows_by_query(_paged_attention(
                        "dsa", page_table, j, _rows_by_class(qb, db, ds, 1, n_heads),
                        _pad_rows(kf.reshape(db, ds, n_heads, hd), page),
                        _pad_rows(vf.reshape(db, ds, n_heads, hd), page),
                        cache_dsa_k, cache_dsa_v, n_pages, sel=jnp.repeat(sel, n_heads, axis=2)), db, ds)
                w_out = w_d_out[j]
            x[n] = _mm_res(y, w_out, x[n], tm)
            fb = _rmsnorm(x[n], g_ffn[i], BF16, tm)
            hist = None if n == "p" else _expand_history(state_ffn_conv[i])
            act, st = _ffn_up(fb, w_ffn_up[i], conv_ffn[i], hist, tm, slen)
            outs["fc"][n].append(_last_two(st, nseq[n]))
            x[n] = _mm_res(act, w_ffn_down[i], x[n], tm)

    y_prompt = _rmsnorm(x["p"], g_final, F32, tm_p).reshape(batch, seq, d)
    y_sample = _rmsnorm(x["s"], g_final, F32, ms).reshape(db, ds, d)
    res = [y_prompt, y_sample]
    for name in ("conv", "sbk", "sbv", "dk", "dv", "ak", "av", "ai", "fc"):
        for n in ("p", "s"):
            res.append(jnp.stack(outs[name][n]))
    return tuple(res)
```

```python
import functools
import math

import jax
import jax.numpy as jnp
from jax import lax
from jax.experimental import pallas as pl
from jax.experimental.pallas import tpu as pltpu

NORM_EPS = 1e-6
ROPE_THETA = 10000.0
TOPK_MAX = 256
CONV_W = 3
LANES = 128
SUBLANES = 8
MXU_WIDTH = 256
FLASH_BLOCK = 512
SB_SKIP_BELOW = -104.0
VMEM_LIMIT_BYTES = 56 * 1024 * 1024
NEG_BIG = -1e30
M_INIT = -1e29
INT_MIN = -2 ** 31

F32 = jnp.float32
BF16 = jnp.bfloat16


def _params(*sem):
    return pltpu.CompilerParams(dimension_semantics=sem, vmem_limit_bytes=VMEM_LIMIT_BYTES)


def _dot(a, b):
    return jnp.dot(a, b, preferred_element_type=F32)


def _dot_t(a, b):
    return lax.dot_general(a, b, (((1,), (1,)), ((), ())), preferred_element_type=F32)


def _col_chunk(tn):
    return MXU_WIDTH if tn % MXU_WIDTH == 0 else LANES


def _rmsnorm_kernel(x_ref, g_ref, o_ref):
    x = x_ref[...]
    ms = jnp.mean(x * x, axis=-1, keepdims=True)
    o_ref[...] = ((x * lax.rsqrt(ms + NORM_EPS)) * g_ref[...]).astype(o_ref.dtype)


def _rmsnorm(x, g, out_dtype, tm):
    m, d = x.shape
    return pl.pallas_call(
        _rmsnorm_kernel,
        grid=(m // tm,),
        in_specs=[pl.BlockSpec((tm, d), lambda i: (i, 0)), pl.BlockSpec((1, d), lambda i: (0, 0))],
        out_specs=pl.BlockSpec((tm, d), lambda i: (i, 0)),
        out_shape=jax.ShapeDtypeStruct((m, d), out_dtype),
        compiler_params=_params("arbitrary"),
    )(x, g.reshape(1, d))


def _rope_group(y, c, s, half):
    if 2 * half == LANES:
        rot = pltpu.roll(y, half, 1)
    else:
        lane = lax.broadcasted_iota(jnp.int32, y.shape, 1)
        rot = jnp.where(lane % (2 * half) < half, pltpu.roll(y, LANES - half, 1), pltpu.roll(y, half, 1))
    return y * c + rot * s


def _proj_kernel(*refs, half, n_out):
    if half:
        x_ref, w_ref, c_ref, s_ref = refs[:4]
        outs = refs[4:4 + n_out]
    else:
        x_ref, w_ref = refs[:2]
        outs = refs[2:2 + n_out]
    wb_ref = refs[-1]

    @pl.when(pl.program_id(1) == 0)
    def _():
        wb_ref[...] = w_ref[...].astype(BF16)

    x = x_ref[...]
    tn = wb_ref.shape[1]
    ch = _col_chunk(tn)
    for c0 in range(0, tn, ch):
        y = _dot(x, wb_ref[:, c0:c0 + ch])
        if half:
            c = c_ref[...]
            s = s_ref[...]
            for g in range(ch // LANES):
                sl = slice(c0 + g * LANES, c0 + (g + 1) * LANES)
                yg = _rope_group(y[:, g * LANES:(g + 1) * LANES], c, s, half)
                for o in outs:
                    o[:, sl] = yg.astype(o.dtype)
        else:
            for o in outs:
                o[:, c0:c0 + ch] = y.astype(o.dtype)


def _proj(xb, w, layer, col0, ncols, out_dtypes, tm, rope=None):
    m, d = xb.shape
    tn = min(1024, ncols)
    assert ncols % tn == 0 and col0 % tn == 0 and m % tm == 0
    off = col0 // tn
    in_specs = [pl.BlockSpec((tm, d), lambda j, i: (i, 0)),
                pl.BlockSpec((None, d, tn), lambda j, i: (layer, 0, j + off))]
    args = [xb, w]
    half = 0
    if rope is not None:
        ctab, stab, half, tblocks = rope
        in_specs += [pl.BlockSpec((tm, LANES), lambda j, i: (i % tblocks, 0)),
                     pl.BlockSpec((tm, LANES), lambda j, i: (i % tblocks, 0))]
        args += [ctab, stab]
    outs = pl.pallas_call(
        functools.partial(_proj_kernel, half=half, n_out=len(out_dtypes)),
        grid=(ncols // tn, m // tm),
        in_specs=in_specs,
        out_specs=[pl.BlockSpec((tm, tn), lambda j, i: (i, j)) for _ in out_dtypes],
        out_shape=[jax.ShapeDtypeStruct((m, ncols), dt) for dt in out_dtypes],
        scratch_shapes=[pltpu.VMEM((d, tn), BF16)],
        compiler_params=_params("arbitrary", "arbitrary"),
    )(*args)
    return outs


def _mm_res_kernel(a_ref, w_ref, r_ref, o_ref, wb_ref):
    @pl.when(pl.program_id(1) == 0)
    def _():
        wb_ref[...] = w_ref[...].astype(BF16)

    o_ref[...] = r_ref[...] + _dot(a_ref[...], wb_ref[...])


def _mm_res(a, w, layer, res, tm):
    m, k = a.shape
    n = w.shape[2]
    tn = min(512, n)
    return pl.pallas_call(
        _mm_res_kernel,
        grid=(n // tn, m // tm),
        in_specs=[pl.BlockSpec((tm, k), lambda j, i: (i, 0)),
                  pl.BlockSpec((None, k, tn), lambda j, i: (layer, 0, j)),
                  pl.BlockSpec((tm, tn), lambda j, i: (i, j))],
        out_specs=pl.BlockSpec((tm, tn), lambda j, i: (i, j)),
        out_shape=jax.ShapeDtypeStruct((m, n), F32),
        scratch_shapes=[pltpu.VMEM((k, tn), BF16)],
        compiler_params=_params("arbitrary", "arbitrary"),
    )(a, w, res)


def _shifted(u, prev, period):
    row = lax.broadcasted_iota(jnp.int32, u.shape, 0)
    if period == 0:
        p1 = prev[SUBLANES - 1:SUBLANES, :]
        p2 = prev[SUBLANES - 2:SUBLANES - 1, :]
        u1 = jnp.where(row >= 1, pltpu.roll(u, 1, 0), p1)
        u2 = jnp.where(row >= 2, pltpu.roll(u, 2, 0), jnp.where(row == 1, p1, p2))
        return u1, u2
    row = row % period
    u1 = jnp.where(row >= 1, pltpu.roll(u, 1, 0), pltpu.roll(prev, 1, 0))
    u2 = jnp.where(row >= 2, pltpu.roll(u, 2, 0), pltpu.roll(prev, 2, 0))
    return u1, u2


def _dwconv(u, u1, u2, cw):
    return (cw[0:1, :] * u2 + cw[1:2, :] * u1) + cw[2:3, :] * u


def _history(prev_ref, hist_ref, tiles_per_seq):
    if hist_ref is not None:
        return hist_ref[...]

    @pl.when(pl.program_id(1) % tiles_per_seq == 0)
    def _():
        prev_ref[...] = jnp.zeros_like(prev_ref)

    return prev_ref[...]


def _convmix_kernel(*refs, short, tiles_per_seq, period):
    if short:
        x_ref, wb_ref, wc_ref, wh_ref, cw_ref, hist_ref, y_ref, st_ref, wbb, wcb, whb, prev_ref = refs
    else:
        x_ref, wb_ref, wc_ref, wh_ref, cw_ref, y_ref, st_ref, wbb, wcb, whb, prev_ref = refs
        hist_ref = None

    @pl.when(pl.program_id(1) == 0)
    def _():
        wbb[...] = wb_ref[...].astype(BF16)
        wcb[...] = wc_ref[...].astype(BF16)
        whb[...] = wh_ref[...].astype(BF16)

    x = x_ref[...]
    prev_all = _history(prev_ref, hist_ref, tiles_per_seq)
    tn = wbb.shape[1]
    ch = _col_chunk(tn)
    for c0 in range(0, tn, ch):
        sl = slice(c0, c0 + ch)
        u = _dot(x, wcb[:, sl]) * _dot(x, whb[:, sl])
        u1, u2 = _shifted(u, prev_all[:, sl], period)
        conv = _dwconv(u, u1, u2, cw_ref[:, sl])
        y_ref[:, sl] = (_dot(x, wbb[:, sl]) * conv).astype(y_ref.dtype)
        if short:
            st_ref[:, sl] = u
        else:
            prev_ref[:, sl] = u[u.shape[0] - SUBLANES:, :]
            st_ref[:, sl] = u[u.shape[0] - SUBLANES:, :]


def _state_specs(short, tm, tn, tiles_per_seq, col_off=0):
    if short:
        return pl.BlockSpec((tm, tn), lambda j, i: (0, j + col_off))
    return pl.BlockSpec((SUBLANES, tn), lambda j, i: (i // tiles_per_seq, j + col_off))


def _seq_mode(hist_x, m, tm, seq_len):
    if hist_x is not None:
        assert seq_len == SUBLANES and tm == m
        return True, 1, seq_len, m // seq_len, tm
    assert seq_len % tm == 0
    return False, seq_len // tm, 0, m // seq_len, SUBLANES


def _convmix(xb, w_in, conv_w, layer, hist_x, tm, seq_len):
    m, d = xb.shape
    tn = min(512, d)
    nj = d // tn
    short, tiles_per_seq, period, nseq, carry_rows = _seq_mode(hist_x, m, tm, seq_len)
    in_specs = [pl.BlockSpec((tm, d), lambda j, i: (i, 0)),
                pl.BlockSpec((None, d, tn), lambda j, i: (layer, 0, j)),
                pl.BlockSpec((None, d, tn), lambda j, i: (layer, 0, j + nj)),
                pl.BlockSpec((None, d, tn), lambda j, i: (layer, 0, j + 2 * nj)),
                pl.BlockSpec((None, CONV_W, tn), lambda j, i: (layer, 0, j))]
    args = [xb, w_in, w_in, w_in, conv_w]
    if short:
        in_specs.append(pl.BlockSpec((tm, tn), lambda j, i: (0, j)))
        args.append(hist_x)
    y, st = pl.pallas_call(
        functools.partial(_convmix_kernel, short=short, tiles_per_seq=tiles_per_seq, period=period),
        grid=(nj, m // tm),
        in_specs=in_specs,
        out_specs=[pl.BlockSpec((tm, tn), lambda j, i: (i, j)), _state_specs(short, tm, tn, tiles_per_seq)],
        out_shape=[jax.ShapeDtypeStruct((m, d), BF16), jax.ShapeDtypeStruct((nseq * SUBLANES, d), F32)],
        scratch_shapes=[pltpu.VMEM((d, tn), BF16)] * 3 + [pltpu.VMEM((carry_rows, tn), F32)],
        compiler_params=_params("arbitrary", "arbitrary"),
    )(*args)
    return y, st


def _ffn_up_kernel(*refs, short, tiles_per_seq, period):
    if short:
        (x_ref, wg_ref, wu_ref, cg_ref, cu_ref, hg_ref, hu_ref,
         a_ref, sg_ref, su_ref, wgb, wub, pg_ref, pu_ref) = refs
    else:
        x_ref, wg_ref, wu_ref, cg_ref, cu_ref, a_ref, sg_ref, su_ref, wgb, wub, pg_ref, pu_ref = refs
        hg_ref = hu_ref = None

    @pl.when(pl.program_id(1) == 0)
    def _():
        wgb[...] = wg_ref[...].astype(BF16)
        wub[...] = wu_ref[...].astype(BF16)

    x = x_ref[...]
    prevs = (_history(pg_ref, hg_ref, tiles_per_seq), _history(pu_ref, hu_ref, tiles_per_seq))
    tn = wgb.shape[1]
    ch = _col_chunk(tn)
    for c0 in range(0, tn, ch):
        sl = slice(c0, c0 + ch)
        halves = []
        for w_b, c_ref, prev, p_ref, s_ref in ((wgb, cg_ref, prevs[0], pg_ref, sg_ref),
                                               (wub, cu_ref, prevs[1], pu_ref, su_ref)):
            up = _dot(x, w_b[:, sl])
            u1, u2 = _shifted(up, prev[:, sl], period)
            halves.append(_dwconv(up, u1, u2, c_ref[:, sl]))
            if short:
                s_ref[:, sl] = up
            else:
                p_ref[:, sl] = up[up.shape[0] - SUBLANES:, :]
                s_ref[:, sl] = up[up.shape[0] - SUBLANES:, :]
        g, u = halves
        a_ref[:, sl] = ((g * jax.nn.sigmoid(g)) * u).astype(a_ref.dtype)


def _ffn_up(xb, w_up, conv_w, layer, hist_x, tm, seq_len):
    m, d = xb.shape
    d_ff = w_up.shape[2] // 2
    tn = 512 if d_ff % 512 == 0 else LANES
    nj = d_ff // tn
    short, tiles_per_seq, period, nseq, carry_rows = _seq_mode(hist_x, m, tm, seq_len)
    in_specs = [pl.BlockSpec((tm, d), lambda j, i: (i, 0)),
                pl.BlockSpec((None, d, tn), lambda j, i: (layer, 0, j)),
                pl.BlockSpec((None, d, tn), lambda j, i: (layer, 0, j + nj)),
                pl.BlockSpec((None, CONV_W, tn), lambda j, i: (layer, 0, j)),
                pl.BlockSpec((None, CONV_W, tn), lambda j, i: (layer, 0, j + nj))]
    args = [xb, w_up, w_up, conv_w, conv_w]
    if short:
        in_specs += [pl.BlockSpec((tm, tn), lambda j, i: (0, j)), pl.BlockSpec((tm, tn), lambda j, i: (0, j + nj))]
        args += [hist_x, hist_x]
    act, sg, su = pl.pallas_call(
        functools.partial(_ffn_up_kernel, short=short, tiles_per_seq=tiles_per_seq, period=period),
        grid=(nj, m // tm),
        in_specs=in_specs,
        out_specs=[pl.BlockSpec((tm, tn), lambda j, i: (i, j)),
                   _state_specs(short, tm, tn, tiles_per_seq),
                   _state_specs(short, tm, tn, tiles_per_seq)],
        out_shape=[jax.ShapeDtypeStruct((m, d_ff), BF16),
                   jax.ShapeDtypeStruct((nseq * SUBLANES, d_ff), F32),
                   jax.ShapeDtypeStruct((nseq * SUBLANES, d_ff), F32)],
        scratch_shapes=[pltpu.VMEM((d, tn), BF16)] * 2 + [pltpu.VMEM((carry_rows, tn), F32)] * 2,
        compiler_params=_params("arbitrary", "arbitrary"),
    )(*args)
    return act, jnp.concatenate([sg, su], axis=1)


def _log_sigmoid_pair(z):
    l1p = jnp.log(1.0 + jnp.exp(-jnp.abs(z)))
    ls = jnp.minimum(z, 0.0) - l1p
    return ls, ls - z


def _suffix_sum(x, tri):
    hi = x.astype(BF16)
    lo = (x - hi.astype(F32)).astype(BF16)
    return _dot(hi, tri) + _dot(lo, tri)


def _tri(n):
    j = lax.broadcasted_iota(jnp.int32, (n, n), 0)
    s = lax.broadcasted_iota(jnp.int32, (n, n), 1)
    return (j > s).astype(BF16)


def _sb_prompt_kernel(q_ref, k_ref, v_ref, tri_ref, o_ref, *, tq, scale):
    qi = pl.program_id(2)
    q = q_ref[...]
    tri = tri_ref[...]

    def block(ki, carry, acc, diag):
        start = pl.multiple_of(ki * tq, tq)
        k = k_ref[pl.ds(start, tq), :]
        v = v_ref[pl.ds(start, tq), :]
        z = _dot_t(q, k) * scale
        ls, l1m = _log_sigmoid_pair(z)
        if diag:
            row = lax.broadcasted_iota(jnp.int32, z.shape, 0)
            col = lax.broadcasted_iota(jnp.int32, z.shape, 1)
            valid = col < row
            l1m = jnp.where(valid, l1m, 0.0)
        after = _suffix_sum(l1m, tri)
        a = jnp.exp(ls + after + carry)
        if diag:
            a = jnp.where(valid, a, 0.0)
        acc = acc + _dot(a.astype(BF16), v)
        carry = carry + jnp.sum(l1m, axis=1, keepdims=True)
        return carry, acc

    carry0 = jnp.zeros((tq, 1), F32)
    acc0 = jnp.zeros((tq, v_ref.shape[1]), F32)
    carry, acc = block(qi, carry0, acc0, True)

    def more(c):
        return (c[0] < qi) & (jnp.max(c[1]) > SB_SKIP_BELOW)

    def body(c):
        carry, acc = block(qi - 1 - c[0], c[1], c[2], False)
        return c[0] + 1, carry, acc

    _, carry, acc = lax.while_loop(more, body, (jnp.int32(0), carry, acc))
    o_ref[...] = acc.astype(o_ref.dtype)


def _sb_prompt(qb, kb, vb, batch, seq, n_heads, hd):
    m, d = qb.shape
    tq = min(256, seq)
    nq = seq // tq
    return pl.pallas_call(
        functools.partial(_sb_prompt_kernel, tq=tq, scale=hd ** -0.5),
        grid=(batch, n_heads, nq),
        in_specs=[pl.BlockSpec((tq, hd), lambda b, h, q: (b * nq + q, h)),
                  pl.BlockSpec((seq, hd), lambda b, h, q: (b, h)),
                  pl.BlockSpec((seq, hd), lambda b, h, q: (b, h)),
                  pl.BlockSpec((tq, tq), lambda b, h, q: (0, 0))],
        out_specs=pl.BlockSpec((tq, hd), lambda b, h, q: (b * nq + q, h)),
        out_shape=jax.ShapeDtypeStruct((m, d), BF16),
        compiler_params=_params("arbitrary", "arbitrary", "arbitrary"),
    )(qb, kb, vb, _tri(tq))


def _online_softmax_step(s, mask, m, l, acc, v):
    if mask is not None:
        s = jnp.where(mask, s, NEG_BIG)
    m_new = jnp.maximum(m, jnp.max(s, axis=1, keepdims=True))
    p = jnp.exp(s - m_new)
    if mask is not None:
        p = jnp.where(mask, p, 0.0)
    alpha = jnp.exp(m - m_new)
    l = alpha * l + jnp.sum(p, axis=1, keepdims=True)
    acc = alpha * acc + _dot(p.astype(BF16), v)
    return m_new, l, acc


def _lambda_full(lam_ref, lambda_init):
    lf = lam_ref[...]
    s01 = jnp.sum(lf[0:1, :] * lf[1:2, :], axis=1, keepdims=True)
    s23 = jnp.sum(lf[2:3, :] * lf[3:4, :], axis=1, keepdims=True)
    return jnp.exp(s01) - jnp.exp(s23) + lambda_init


def _sub_norm(o, g, lambda_init):
    ms = jnp.mean(o * o, axis=-1, keepdims=True)
    return ((o * lax.rsqrt(ms + NORM_EPS)) * g) * (1.0 - lambda_init)


def _diff_prompt_kernel(q_ref, k_ref, v_ref, lam_ref, g_ref, o_ref, *, tq, hd, scale, lambda_init):
    qi = pl.program_id(2)
    dv = v_ref.shape[1]

    def block(ki, state, diag):
        start = pl.multiple_of(ki * tq, tq)
        v = v_ref[pl.ds(start, tq), :]
        mask = None
        if diag:
            row = lax.broadcasted_iota(jnp.int32, (tq, tq), 0)
            col = lax.broadcasted_iota(jnp.int32, (tq, tq), 1)
            mask = col <= row
        new = []
        for i in range(2):
            q = q_ref[:, i * hd:(i + 1) * hd]
            k = k_ref[pl.ds(start, tq), i * hd:(i + 1) * hd]
            s = _dot_t(q, k) * scale
            new.append(_online_softmax_step(s, mask, *state[i], v))
        return tuple(new)

    init = tuple((jnp.full((tq, 1), NEG_BIG, F32), jnp.zeros((tq, 1), F32), jnp.zeros((tq, dv), F32))
                 for _ in range(2))
    state = lax.fori_loop(0, qi, lambda ki, st: block(ki, st, False), init)
    state = block(qi, state, True)
    lam = _lambda_full(lam_ref, lambda_init)
    (_, l0, a0), (_, l1, a1) = state
    o = a0 / l0 - lam * (a1 / l1)
    o_ref[...] = _sub_norm(o, g_ref[...], lambda_init).astype(o_ref.dtype)


def _diff_prompt(qb, kb, vb, lam, g_sub, batch, seq, n_dh, hd, lambda_init):
    m, d = qb.shape
    tq = min(FLASH_BLOCK, seq)
    nq = seq // tq
    w = 2 * hd
    return pl.pallas_call(
        functools.partial(_diff_prompt_kernel, tq=tq, hd=hd, scale=hd ** -0.5, lambda_init=lambda_init),
        grid=(batch, n_dh, nq),
        in_specs=[pl.BlockSpec((tq, w), lambda b, h, q: (b * nq + q, h)),
                  pl.BlockSpec((seq, w), lambda b, h, q: (b, h)),
                  pl.BlockSpec((seq, w), lambda b, h, q: (b, h)),
                  pl.BlockSpec(lam.shape, lambda b, h, q: (0, 0)),
                  pl.BlockSpec((1, w), lambda b, h, q: (0, 0))],
        out_specs=pl.BlockSpec((tq, w), lambda b, h, q: (b * nq + q, h)),
        out_shape=jax.ShapeDtypeStruct((m, d), BF16),
        compiler_params=_params("arbitrary", "arbitrary", "arbitrary"),
    )(qb, kb, vb, lam, g_sub.reshape(1, w))


def _dsa_prompt_kernel(q_ref, k_ref, v_ref, sel_ref, o_ref, *, tq, scale):
    qi = pl.program_id(2)
    q = q_ref[...]
    dv = v_ref.shape[1]

    def block(ki, state):
        start = pl.multiple_of(ki * tq, tq)
        k = k_ref[pl.ds(start, tq), :]
        v = v_ref[pl.ds(start, tq), :]
        s = _dot_t(q, k) * scale + sel_ref[:, pl.ds(start, tq)].astype(F32)
        return _online_softmax_step(s, None, *state, v)

    init = (jnp.full((tq, 1), M_INIT, F32), jnp.zeros((tq, 1), F32), jnp.zeros((tq, dv), F32))
    _, l, acc = lax.fori_loop(0, qi + 1, block, init)
    o_ref[...] = (acc / l).astype(o_ref.dtype)


def _dsa_prompt(qb, kb, vb, sel, batch, seq, n_heads, hd):
    m, d = qb.shape
    tq = min(FLASH_BLOCK, seq)
    nq = seq // tq
    return pl.pallas_call(
        functools.partial(_dsa_prompt_kernel, tq=tq, scale=hd ** -0.5),
        grid=(batch, n_heads, nq),
        in_specs=[pl.BlockSpec((tq, hd), lambda b, h, q: (b * nq + q, h)),
                  pl.BlockSpec((seq, hd), lambda b, h, q: (b, h)),
                  pl.BlockSpec((seq, hd), lambda b, h, q: (b, h)),
                  pl.BlockSpec((tq, seq), lambda b, h, q: (b * nq + q, 0))],
        out_specs=pl.BlockSpec((tq, hd), lambda b, h, q: (b * nq + q, h)),
        out_shape=jax.ShapeDtypeStruct((m, d), BF16),
        compiler_params=_params("arbitrary", "arbitrary", "arbitrary"),
    )(qb, kb, vb, sel)


def _order_key(x):
    bits = pltpu.bitcast(x + 0.0, jnp.int32)
    return jnp.where(bits >= 0, bits, bits ^ jnp.int32(0x7FFFFFFF))


def _kth_largest_key(count_ge, rows, k):
    zero = jnp.zeros((rows, 1), jnp.int32)
    t0 = jnp.where(count_ge(zero) >= k, zero, jnp.int32(INT_MIN))

    def body(it, t):
        cand = t | jnp.left_shift(jnp.int32(1), jnp.int32(30) - it)
        return jnp.where(count_ge(cand) >= k, cand, t)

    return lax.fori_loop(0, 31, body, t0)


def _store_topk_mask(o_ref, key, valid, k, t):
    ge = valid & (key >= t)
    n_ge = jnp.sum(ge.astype(jnp.int32), axis=1, keepdims=True)
    has_ties = jnp.max(n_ge) > k

    @pl.when(jnp.logical_not(has_ties))
    def _():
        o_ref[...] = jnp.where(ge, 0.0, NEG_BIG).astype(o_ref.dtype)

    @pl.when(has_ties)
    def _():
        gt = key > t
        eq = key == t
        need = k - jnp.sum(gt.astype(jnp.int32), axis=1, keepdims=True)
        col = lax.broadcasted_iota(jnp.int32, key.shape, 1)
        nbits = int(key.shape[1]).bit_length()

        def body(it, j):
            cand = j | jnp.left_shift(jnp.int32(1), jnp.int32(nbits - 1) - it)
            c = jnp.sum((eq & (col < cand)).astype(jnp.int32), axis=1, keepdims=True)
            return jnp.where(c < need, cand, j)

        j = lax.fori_loop(0, nbits, body, jnp.zeros_like(t))
        sel = valid & (gt | (eq & (col <= j)))
        o_ref[...] = jnp.where(sel, 0.0, NEG_BIG).astype(o_ref.dtype)


def _select_prompt_kernel(iq_ref, ik_ref, iw_ref, o_ref, key_ref, qh_ref, *, tq, cw, n_ih, idim, k_top, wscale):
    qi = pl.program_id(1)
    seq = key_ref.shape[1]
    n_need = (qi * tq) // cw + 1
    iw = iw_ref[...] * wscale
    lane = lax.broadcasted_iota(jnp.int32, (tq, LANES), 1)
    per_group = LANES // idim
    for h in range(n_ih):
        g, r = divmod(h, per_group)
        qg = iq_ref[:, g * LANES:(g + 1) * LANES].astype(F32)
        qh_ref[h] = jnp.where((lane >= r * idim) & (lane < (r + 1) * idim), qg, 0.0).astype(BF16)

    def chunk_at(c):
        return pl.ds(pl.multiple_of(c * cw, cw), cw)

    def score_chunk(c, carry):
        ikc = ik_ref[chunk_at(c), :]
        acc = None
        for h in range(n_ih):
            s = jnp.maximum(_dot_t(qh_ref[h], ikc), 0.0) * iw[:, idim + h:idim + h + 1]
            acc = s if acc is None else acc + s
        row = qi * tq + lax.broadcasted_iota(jnp.int32, (tq, cw), 0)
        col = c * cw + lax.broadcasted_iota(jnp.int32, (tq, cw), 1)
        key_ref[:, chunk_at(c)] = _order_key(jnp.where(col <= row, acc, -jnp.inf))
        return carry

    def fill_chunk(c, carry):
        key_ref[:, chunk_at(c)] = jnp.full((tq, cw), INT_MIN, jnp.int32)
        return carry

    lax.fori_loop(0, n_need, score_chunk, 0)
    lax.fori_loop(n_need, seq // cw, fill_chunk, 0)

    def count_ge(t):
        def body(c, acc):
            m = (key_ref[:, chunk_at(c)] >= t).astype(jnp.int32)
            for v in range(cw // LANES):
                acc = acc + m[:, v * LANES:(v + 1) * LANES]
            return acc

        acc = lax.fori_loop(0, n_need, body, jnp.zeros((tq, LANES), jnp.int32))
        return jnp.sum(acc, axis=1, keepdims=True)

    t = _kth_largest_key(count_ge, tq, k_top)
    row = qi * tq + lax.broadcasted_iota(jnp.int32, (tq, seq), 0)
    col = lax.broadcasted_iota(jnp.int32, (tq, seq), 1)
    _store_topk_mask(o_ref, key_ref[...], col <= row, k_top, t)


def _select_prompt(iqb, ik2b, tail, batch, seq, n_ih, idim, k_top):
    m = iqb.shape[0]
    tq = min(128, seq)
    cw = min(512, seq)
    nq = seq // tq
    assert LANES % idim == 0 and cw % tq == 0 and seq % cw == 0
    return pl.pallas_call(
        functools.partial(_select_prompt_kernel, tq=tq, cw=cw, n_ih=n_ih, idim=idim, k_top=k_top,
                          wscale=(idim ** -0.5) * (n_ih ** -0.5)),
        grid=(batch, nq),
        in_specs=[pl.BlockSpec((tq, n_ih * idim), lambda b, q: (b * nq + q, 0)),
                  pl.BlockSpec((seq, LANES), lambda b, q: (b, 0)),
                  pl.BlockSpec((tq, LANES), lambda b, q: (b * nq + q, 0))],
        out_specs=pl.BlockSpec((tq, seq), lambda b, q: (b * nq + q, 0)),
        out_shape=jax.ShapeDtypeStruct((m, seq), BF16),
        scratch_shapes=[pltpu.VMEM((tq, seq), jnp.int32), pltpu.VMEM((n_ih, tq, LANES), BF16)],
        compiler_params=_params("arbitrary", "arbitrary"),
    )(iqb, ik2b, tail)


def _sel_sample_kernel(pt_ref, qm_ref, w_ref, iknew_ref, rep_ref, *refs, pages_per_step, n_pages, ds, k_top):
    ik_refs = refs[:pages_per_step]
    o_ref, sc_ref, sel_ref = refs[pages_per_step:pages_per_step + 3]
    s_id = pl.program_id(1)
    qm = qm_ref[...]
    w = w_ref[...]
    page = ik_refs[0].shape[0]

    def scores(ik):
        s = jnp.maximum(_dot_t(qm, ik.astype(BF16)), 0.0) * w
        tot = s[0:ds, :]
        for h in range(1, s.shape[0] // ds):
            tot = tot + s[h * ds:(h + 1) * ds, :]
        return tot

    for c in range(pages_per_step):
        pg = s_id * pages_per_step + c
        sc_ref[:, pl.ds(pl.multiple_of(pg * page, page), page)] = scores(ik_refs[c][...])

    @pl.when(s_id == pl.num_programs(1) - 1)
    def _():
        t = lax.broadcasted_iota(jnp.int32, (ds, page), 0)
        j = lax.broadcasted_iota(jnp.int32, (ds, page), 1)
        sc_ref[:, n_pages * page:] = jnp.where(j <= t, scores(iknew_ref[...]), -jnp.inf)
        col = lax.broadcasted_iota(jnp.int32, sc_ref.shape, 1)
        row = lax.broadcasted_iota(jnp.int32, sc_ref.shape, 0)
        key = _order_key(sc_ref[...])

        def count_ge(thr):
            return jnp.sum((key >= thr).astype(jnp.int32), axis=1, keepdims=True)

        thr = _kth_largest_key(count_ge, ds, k_top)
        _store_topk_mask(sel_ref, key, col <= n_pages * page + row, k_top, thr)
        rep = rep_ref[...]
        wide = rep.shape[1]

        def widen(c, carry):
            src = sel_ref[:, pl.ds(pl.multiple_of(c * page, page), page)]
            o_ref[:, pl.ds(pl.multiple_of(c * wide, wide), wide)] = _dot(src, rep).astype(o_ref.dtype)
            return carry

        lax.fori_loop(0, n_pages + 1, widen, 0)


def _select_sample(page_table, qm, wcol, iknew, pool, layer, n_pages, k_top, n_rep):
    db, rows, idim = qm.shape
    page = pool.shape[2]
    ds = SUBLANES
    pps = 8 if n_pages % 8 == 0 else (4 if n_pages % 4 == 0 else 1)
    n_steps = n_pages // pps
    nk = (n_pages + 1) * page
    rep = jnp.repeat(jnp.eye(page, dtype=BF16), n_rep, axis=1)

    def pool_spec(c):
        return pl.BlockSpec((None, None, page, idim), lambda b, s, pt: (layer, pt[b, s * pps + c], 0, 0))

    grid_spec = pltpu.PrefetchScalarGridSpec(
        num_scalar_prefetch=1,
        grid=(db, n_steps),
        in_specs=[pl.BlockSpec((None, rows, idim), lambda b, s, pt: (b, 0, 0)),
                  pl.BlockSpec((None, rows, 1), lambda b, s, pt: (b, 0, 0)),
                  pl.BlockSpec((None, page, idim), lambda b, s, pt: (b, 0, 0)),
                  pl.BlockSpec(rep.shape, lambda b, s, pt: (0, 0))]
                 + [pool_spec(c) for c in range(pps)],
        out_specs=pl.BlockSpec((None, ds, nk * n_rep), lambda b, s, pt: (b, 0, 0)),
        scratch_shapes=[pltpu.VMEM((ds, nk), F32), pltpu.VMEM((ds, nk), BF16)],
    )
    return pl.pallas_call(
        functools.partial(_sel_sample_kernel, pages_per_step=pps, n_pages=n_pages, ds=ds, k_top=k_top),
        grid_spec=grid_spec,
        out_shape=jax.ShapeDtypeStruct((db, ds, nk * n_rep), BF16),
        compiler_params=_params("arbitrary", "arbitrary"),
    )(page_table, qm, wcol, iknew, rep, *([pool] * pps))


def _class_reduce(x, n_cls, op):
    r = x[:, 0:LANES]
    for v in range(1, x.shape[1] // LANES):
        r = op(r, x[:, v * LANES:(v + 1) * LANES])
    sh = LANES // 2
    while sh >= n_cls:
        r = op(r, pltpu.roll(r, sh, 1))
        sh //= 2
    return r


def _tile_lanes(r, ncol):
    return jnp.concatenate([r] * (ncol // LANES), axis=1)


def _class_column(r, n_cls):
    return jnp.concatenate([r[:, c:c + 1] for c in range(n_cls)], axis=0)


def _suffix_by_class(x, n_cls):
    lane = lax.broadcasted_iota(jnp.int32, (x.shape[0], LANES), 1)
    nv = x.shape[1] // LANES
    excl_in, tots = [], []
    for v in range(nv):
        xv = x[:, v * LANES:(v + 1) * LANES]
        inc = xv
        sh = n_cls
        while sh < LANES:
            inc = inc + jnp.where(lane < LANES - sh, pltpu.roll(inc, LANES - sh, 1), 0.0)
            sh *= 2
        tot = xv
        sh = LANES // 2
        while sh >= n_cls:
            tot = tot + pltpu.roll(tot, sh, 1)
            sh //= 2
        excl_in.append(inc - xv)
        tots.append(tot)
    later = jnp.zeros((x.shape[0], LANES), F32)
    out = [None] * nv
    for v in reversed(range(nv)):
        out[v] = excl_in[v] + later
        later = later + tots[v]
    return jnp.concatenate(out, axis=1), later


def _paged_kernel(pt_ref, *refs, kind, pages_per_step, n_cls, n_grp, scale, lambda_init):
    n_in = 3 + (2 if kind == "dsa" else 0) + 2 * pages_per_step + (2 if kind == "diff" else 0)
    ins, o_ref, (m_ref, l_ref, acc_ref) = refs[:n_in], refs[n_in], refs[n_in + 1:]
    q_ref, knew_ref, vnew_ref = ins[:3]
    pos = 3
    if kind == "dsa":
        selnew_ref, sel_ref = ins[pos:pos + 2]
        pos += 2
    k_refs = ins[pos:pos + pages_per_step]
    v_refs = ins[pos + pages_per_step:pos + 2 * pages_per_step]
    pos += 2 * pages_per_step
    if kind == "diff":
        lam_ref, g_ref = ins[pos:pos + 2]
    s_id = pl.program_id(1)
    page = knew_ref.shape[0]
    ncol = page * n_cls
    col = lax.broadcasted_iota(jnp.int32, (SUBLANES, ncol), 1)
    cls = col % n_cls

    def fold(s_big):
        out = s_big[0:SUBLANES, :]
        for c in range(1, n_cls):
            out = jnp.where(cls == c, s_big[c * SUBLANES:(c + 1) * SUBLANES, :], out)
        return out

    def unfold(p):
        return jnp.concatenate([jnp.where(cls == c, p, 0.0) for c in range(n_cls)], axis=0).astype(BF16)

    def process(k_ref, v_ref, mask, bias=None):
        v = v_ref[...]
        v2 = v.reshape(ncol, v.shape[-1]).astype(BF16)
        for g in range(n_grp):
            k = k_ref[...] if n_grp == 1 else k_ref[:, pl.ds(g, n_cls, stride=n_grp), :]
            k2 = k.reshape(ncol, k.shape[-1]).astype(BF16)
            s = fold(_dot_t(q_ref[g], k2) * scale)
            if kind == "sb":
                ls, l1m = _log_sigmoid_pair(s)
                if mask is not None:
                    l1m = jnp.where(mask, l1m, 0.0)
                excl, total = _suffix_by_class(l1m, n_cls)
                a = jnp.exp(ls + excl + _tile_lanes(m_ref[g], ncol))
                if mask is not None:
                    a = jnp.where(mask, a, 0.0)
                acc_ref[g] += _dot(unfold(a), v2)
                m_ref[g] += total
            else:
                if bias is not None:
                    s = s + bias
                if mask is not None:
                    s = jnp.where(mask, s, NEG_BIG)
                m_old = m_ref[g]
                m_new = jnp.maximum(m_old, _class_reduce(s, n_cls, jnp.maximum))
                p = jnp.exp(s - _tile_lanes(m_new, ncol))
                if mask is not None:
                    p = jnp.where(mask, p, 0.0)
                alpha = jnp.exp(m_old - m_new)
                l_ref[g] = alpha * l_ref[g] + _class_reduce(p, n_cls, jnp.add)
                m_ref[g] = m_new
                acc_ref[g] = acc_ref[g] * _class_column(alpha, n_cls) + _dot(unfold(p), v2)

    @pl.when(s_id == 0)
    def _():
        acc_ref[...] = jnp.zeros_like(acc_ref)
        l_ref[...] = jnp.zeros_like(l_ref)
        m_ref[...] = jnp.full_like(m_ref, {"sb": 0.0, "diff": NEG_BIG, "dsa": M_INIT}[kind])
        t = lax.broadcasted_iota(jnp.int32, (SUBLANES, ncol), 0)
        slot = col // n_cls
        if kind == "sb":
            process(knew_ref, vnew_ref, slot < t)
        elif kind == "diff":
            process(knew_ref, vnew_ref, slot <= t)
        else:
            process(knew_ref, vnew_ref, None, selnew_ref[...].astype(F32))

    for c in range(pages_per_step):
        bias = None
        if kind == "dsa":
            lo = (pages_per_step - 1 - c) * ncol
            bias = sel_ref[:, lo:lo + ncol].astype(F32)
        process(k_refs[c], v_refs[c], None, bias)

    @pl.when(s_id == pl.num_programs(1) - 1)
    def _():
        if kind == "sb":
            o_ref[...] = acc_ref[0].astype(o_ref.dtype)
        elif kind == "dsa":
            o_ref[...] = (acc_ref[0] / _class_column(l_ref[0], n_cls)).astype(o_ref.dtype)
        else:
            lam = _lambda_full(lam_ref, lambda_init)
            a0 = acc_ref[0] / _class_column(l_ref[0], n_cls)
            a1 = acc_ref[1] / _class_column(l_ref[1], n_cls)
            o_ref[...] = _sub_norm(a0 - lam * a1, g_ref[...], lambda_init).astype(o_ref.dtype)


def _paged_attention(kind, page_table, layer, q, knew, vnew, kpool, vpool, n_pages, *,
                     sel=None, lam=None, g_sub=None, lambda_init=0.0):
    db, n_grp, rows, hd = q.shape
    n_cls = rows // SUBLANES
    page = kpool.shape[2]
    dv = vpool.shape[4]
    ncol = page * n_cls
    pps = 8 if n_pages % 8 == 0 else (4 if n_pages % 4 == 0 else 1)
    n_steps = n_pages // pps

    def pool_spec(pool, c):
        return pl.BlockSpec((None, None) + pool.shape[2:],
                            lambda b, s, pt: (layer, pt[b, n_pages - 1 - (s * pps + c)], 0, 0, 0))

    in_specs = [pl.BlockSpec((None,) + q.shape[1:], lambda b, s, pt: (b, 0, 0, 0)),
                pl.BlockSpec((None,) + knew.shape[1:], lambda b, s, pt: (b, 0, 0, 0)),
                pl.BlockSpec((None,) + vnew.shape[1:], lambda b, s, pt: (b, 0, 0, 0))]
    args = [q, knew, vnew]
    if kind == "dsa":
        in_specs += [pl.BlockSpec((None, SUBLANES, ncol), lambda b, s, pt: (b, 0, n_pages)),
                     pl.BlockSpec((None, SUBLANES, pps * ncol), lambda b, s, pt: (b, 0, n_steps - 1 - s))]
        args += [sel, sel]
    in_specs += [pool_spec(kpool, c) for c in range(pps)] + [pool_spec(vpool, c) for c in range(pps)]
    args += [kpool] * pps + [vpool] * pps
    if kind == "diff":
        in_specs += [pl.BlockSpec(lam.shape, lambda b, s, pt: (0, 0)),
                     pl.BlockSpec((1, dv), lambda b, s, pt: (0, 0))]
        args += [lam, g_sub.reshape(1, dv)]
    grid_spec = pltpu.PrefetchScalarGridSpec(
        num_scalar_prefetch=1,
        grid=(db, n_steps),
        in_specs=in_specs,
        out_specs=pl.BlockSpec((None, rows, dv), lambda b, s, pt: (b, 0, 0)),
        scratch_shapes=[pltpu.VMEM((n_grp, SUBLANES, LANES), F32), pltpu.VMEM((n_grp, SUBLANES, LANES), F32),
                        pltpu.VMEM((n_grp, rows, dv), F32)],
    )
    return pl.pallas_call(
        functools.partial(_paged_kernel, kind=kind, pages_per_step=pps, n_cls=n_cls, n_grp=n_grp,
                          scale=hd ** -0.5, lambda_init=lambda_init),
        grid_spec=grid_spec,
        out_shape=jax.ShapeDtypeStruct((db, rows, dv), BF16),
        compiler_params=_params("arbitrary", "arbitrary"),
    )(page_table, *args)


def _rope_tables(pos, half, ident_from=None):
    inv = ROPE_THETA ** (-jnp.arange(half, dtype=F32) / half)
    ang = pos.astype(F32)[:, None] * inv
    cos, sin = jnp.cos(ang), jnp.sin(ang)
    reps = LANES // (2 * half)
    c = jnp.tile(jnp.concatenate([cos, cos], axis=1), (1, reps))
    s = jnp.tile(jnp.concatenate([-sin, sin], axis=1), (1, reps))
    if ident_from is not None:
        lane = jnp.arange(LANES)[None, :]
        c = jnp.where(lane < ident_from, c, 1.0)
        s = jnp.where(lane < ident_from, s, 0.0)
    return c, s


def _expand_history(hist):
    nseq, _, c = hist.shape
    z = jnp.zeros((nseq, SUBLANES - 2, c), hist.dtype)
    blocks = jnp.concatenate([z, hist], axis=1).reshape(nseq * SUBLANES, c)
    return jnp.roll(blocks, -SUBLANES, axis=0)


def _last_two(st, nseq):
    return st.reshape(nseq, SUBLANES, st.shape[1])[:, SUBLANES - 2:, :]


def _pad_rows(x, rows):
    return jnp.pad(x, ((0, 0), (0, rows - x.shape[1])) + ((0, 0),) * (x.ndim - 2))


def _rows_by_class(x, db, ds, n_grp, n_cls):
    hd = x.shape[1] // (n_cls * n_grp)
    return x.reshape(db, ds, n_cls, n_grp, hd).transpose(0, 3, 2, 1, 4).reshape(db, n_grp, n_cls * ds, hd)


def _rows_by_query(y, db, ds):
    n_cls, dv = y.shape[1] // ds, y.shape[2]
    return y.reshape(db, n_cls, ds, dv).transpose(0, 2, 1, 3).reshape(db * ds, n_cls * dv)


def kernel(x_prompt, x_sample, state_conv_mix, cache_sb_k, cache_sb_v, cache_diff_k, cache_diff_v, cache_dsa_k, cache_dsa_v, cache_dsa_idx_k, state_ffn_conv, page_table, g_mix, g_ffn, g_final, w_a_in, conv_a, w_a_out, w_b_qkv, w_b_out, w_c_qkv, lam_c, g_c_subln, w_c_out, w_d_in, w_d_out, w_ffn_up, conv_ffn, w_ffn_down):
    batch, seq, d = x_prompt.shape
    db, ds, _ = x_sample.shape
    depth = g_mix.shape[0]
    n_pool, page = cache_sb_k.shape[1], cache_sb_k.shape[2]
    n_heads, hd = cache_sb_k.shape[3], cache_sb_k.shape[4]
    n_dh = cache_diff_k.shape[3]
    idim = cache_dsa_idx_k.shape[3]
    n_pages = page_table.shape[1]
    past = n_pages * page
    n_ih = (w_d_in.shape[2] - 3 * d - idim) // (idim + 1)
    assert ds == SUBLANES and hd == LANES and cache_diff_k.shape[5] == hd and n_dh * 2 == n_heads
    assert (n_ih * idim) % LANES == 0 and idim + n_ih <= LANES

    mp, ms = batch * seq, db * ds
    tm_p = min(512, seq)
    streams = (("p", tm_p, seq), ("s", ms, ds))
    x = {"p": x_prompt.reshape(mp, d), "s": x_sample.reshape(ms, d)}
    pos = {"p": jnp.arange(seq), "s": past + jnp.arange(ds)}
    pos_rows = {"p": pos["p"], "s": jnp.tile(pos["s"], db)}
    rope_blocks = {"p": seq // tm_p, "s": 1}
    rope_hd = {n: _rope_tables(pos_rows[n], hd // 2) for n in x}
    rope_idx = {n: _rope_tables(pos_rows[n], idim // 2) for n in x}
    rope_tail = {n: _rope_tables(pos_rows[n], idim // 2, ident_from=idim) for n in x}
    k_top = {"p": min(TOPK_MAX, seq // 4), "s": min(TOPK_MAX, (past + ds) // 4)}
    nseq = {"p": batch, "s": db}

    outs = {name: {"p": [], "s": []} for name in
            ("conv", "sbk", "sbv", "dk", "dv", "ak", "av", "ai", "fc")}

    def shaped(a, n, tail):
        return a.reshape((nseq[n], seq if n == "p" else ds) + tail)

    for i in range(depth):
        kind, j = i % 4, i // 4
        for n, tm, slen in streams:
            xb = _rmsnorm(x[n], g_mix[i], BF16, tm)
            if kind == 0:
                hist = None if n == "p" else _expand_history(state_conv_mix[j])
                y, st = _convmix(xb, w_a_in, conv_a, j, hist, tm, slen)
                outs["conv"][n].append(_last_two(st, nseq[n]))
                w_out = w_a_out
            elif kind == 1:
                w = w_b_qkv
                (qb,) = _proj(xb, w, j, 0, d, (BF16,), tm)
                kf, kb = _proj(xb, w, j, d, d, (F32, BF16), tm)
                vf, vb = _proj(xb, w, j, 2 * d, d, (F32, BF16), tm)
                outs["sbk"][n].append(shaped(kf, n, (n_heads, hd)))
                outs["sbv"][n].append(shaped(vf, n, (n_heads, hd)))
                if n == "p":
                    y = _sb_prompt(qb, kb, vb, batch, seq, n_heads, hd)
                else:
                    y = _rows_by_query(_paged_attention(
                        "sb", page_table, j, _rows_by_class(qb, db, ds, 1, n_heads),
                        _pad_rows(kf.reshape(db, ds, n_heads, hd), page),
                        _pad_rows(vf.reshape(db, ds, n_heads, hd), page),
                        cache_sb_k, cache_sb_v, n_pages), db, ds)
                w_out = w_b_out
            elif kind == 2:
                lambda_init = 0.8 - 0.6 * math.exp(-0.3 * i)
                w = w_c_qkv
                rope = rope_hd[n] + (hd // 2, rope_blocks[n])
                (qb,) = _proj(xb, w, j, 0, d, (BF16,), tm, rope)
                kf, kb = _proj(xb, w, j, d, d, (F32, BF16), tm, rope)
                vf, vb = _proj(xb, w, j, 2 * d, d, (F32, BF16), tm)
                outs["dk"][n].append(shaped(kf, n, (n_dh, 2, hd)))
                outs["dv"][n].append(shaped(vf, n, (n_dh, 2 * hd)))
                if n == "p":
                    y = _diff_prompt(qb, kb, vb, lam_c[j], g_c_subln[j], batch, seq, n_dh, hd, lambda_init)
                else:
                    y = _rows_by_query(_paged_attention(
                        "diff", page_table, j, _rows_by_class(qb, db, ds, 2, n_dh),
                        _pad_rows(kf.reshape(db, ds, n_heads, hd), page),
                        _pad_rows(vf.reshape(db, ds, n_dh, 2 * hd), page),
                        cache_diff_k.reshape(cache_diff_k.shape[:3] + (n_heads, hd)), cache_diff_v, n_pages,
                        lam=lam_c[j], g_sub=g_c_subln[j], lambda_init=lambda_init), db, ds)
                w_out = w_c_out
            else:
                w = w_d_in
                rope = rope_hd[n] + (hd // 2, rope_blocks[n])
                (qb,) = _proj(xb, w, j, 0, d, (BF16,), tm, rope)
                kf, kb = _proj(xb, w, j, d, d, (F32, BF16), tm, rope)
                vf, vb = _proj(xb, w, j, 2 * d, d, (F32, BF16), tm)
                (iqb,) = _proj(xb, w, j, 3 * d, n_ih * idim, (BF16,), tm, rope_idx[n] + (idim // 2, rope_blocks[n]))
                o2 = 3 * d + n_ih * idim
                w_ik, w_iw = w[j:j + 1, :, o2:o2 + idim], w[j:j + 1, :, o2 + idim:]
                w_tail = jnp.concatenate([w_ik, w_iw, jnp.zeros((1, d, LANES - idim - n_ih), F32)], axis=2)
                (tail,) = _proj(xb, w_tail, 0, 0, LANES, (F32,), tm, rope_tail[n] + (idim // 2, rope_blocks[n]))
                outs["ak"][n].append(shaped(kf, n, (n_heads, hd)))
                outs["av"][n].append(shaped(vf, n, (n_heads, hd)))
                outs["ai"][n].append(shaped(tail[:, :idim], n, (idim,)))
                if n == "p":
                    w_ik2 = jnp.concatenate([w_ik] * (LANES // idim), axis=2)
                    (ik2b,) = _proj(xb, w_ik2, 0, 0, LANES, (BF16,), tm, rope_idx[n] + (idim // 2, rope_blocks[n]))
                    sel = _select_prompt(iqb, ik2b, tail, batch, seq, n_ih, idim, k_top[n])
                    y = _dsa_prompt(qb, kb, vb, sel, batch, seq, n_heads, hd)
                else:
                    qm = iqb.reshape(db, ds, n_ih, idim).transpose(0, 2, 1, 3).reshape(db, n_ih * ds, idim)
                    wcol = tail[:, idim:idim + n_ih].reshape(db, ds, n_ih).transpose(0, 2, 1).reshape(db, n_ih * ds, 1)
                    wcol = wcol * ((idim ** -0.5) * (n_ih ** -0.5))
                    iknew = _pad_rows(tail[:, :idim].reshape(db, ds, idim), page)
                    sel = _select_sample(page_table, qm, wcol, iknew, cache_dsa_idx_k, j, n_pages, k_top[n], n_heads)
                    y = _rows_by_query(_paged_attention(
                        "dsa", page_table, j, _rows_by_class(qb, db, ds, 1, n_heads),
                        _pad_rows(kf.reshape(db, ds, n_heads, hd), page),
                        _pad_rows(vf.reshape(db, ds, n_heads, hd), page),
                        cache_dsa_k, cache_dsa_v, n_pages, sel=sel), db, ds)
                w_out = w_d_out
            x[n] = _mm_res(y, w_out, j, x[n], tm)
            fb = _rmsnorm(x[n], g_ffn[i], BF16, tm)
            hist = None if n == "p" else _expand_history(state_ffn_conv[i])
            act, st = _ffn_up(fb, w_ffn_up, conv_ffn, i, hist, tm, slen)
            outs["fc"][n].append(_last_two(st, nseq[n]))
            x[n] = _mm_res(act, w_ffn_down, i, x[n], tm)

    y_prompt = _rmsnorm(x["p"], g_final, F32, tm_p).reshape(batch, seq, d)
    y_sample = _rmsnorm(x["s"], g_final, F32, ms).reshape(db, ds, d)
    res = [y_prompt, y_sample]
    for name in ("conv", "sbk", "sbv", "dk", "dv", "ak", "av", "ai", "fc"):
        for n in ("p", "s"):
            res.append(jnp.stack(outs[name][n]))
    return tuple(res)
```

```python
import functools
import math

import jax
import jax.numpy as jnp
from jax import lax
from jax.experimental import pallas as pl
from jax.experimental.pallas import tpu as pltpu

NORM_EPS = 1e-6
ROPE_THETA = 10000.0
TOPK_MAX = 256
CONV_W = 3
LANES = 128
SUBLANES = 8
MXU_WIDTH = 256
FLASH_BLOCK = 512
ROW_BLOCK = 512
SB_SKIP_BELOW = -104.0
VMEM_LIMIT_BYTES = 56 * 1024 * 1024
NEG_BIG = -1e30
M_INIT = -1e29
INT_MIN = -2 ** 31

F32 = jnp.float32
BF16 = jnp.bfloat16


def _params(*sem):
    return pltpu.CompilerParams(dimension_semantics=sem, vmem_limit_bytes=VMEM_LIMIT_BYTES)


def _dot(a, b):
    return jnp.dot(a, b, preferred_element_type=F32)


def _dot_t(a, b):
    return lax.dot_general(a, b, (((1,), (1,)), ((), ())), preferred_element_type=F32)


def _col_chunk(tn):
    return MXU_WIDTH if tn % MXU_WIDTH == 0 else LANES


def _rmsnorm_kernel(x_ref, g_ref, o_ref):
    x = x_ref[...]
    ms = jnp.mean(x * x, axis=-1, keepdims=True)
    o_ref[...] = ((x * lax.rsqrt(ms + NORM_EPS)) * g_ref[...]).astype(o_ref.dtype)


def _rmsnorm(x, g, out_dtype, tm):
    m, d = x.shape
    return pl.pallas_call(
        _rmsnorm_kernel,
        grid=(m // tm,),
        in_specs=[pl.BlockSpec((tm, d), lambda i: (i, 0)), pl.BlockSpec((1, d), lambda i: (0, 0))],
        out_specs=pl.BlockSpec((tm, d), lambda i: (i, 0)),
        out_shape=jax.ShapeDtypeStruct((m, d), out_dtype),
        compiler_params=_params("arbitrary"),
    )(x, g.reshape(1, d))


def _rope_group(y, c, s, half):
    if 2 * half == LANES:
        rot = pltpu.roll(y, half, 1)
    else:
        lane = lax.broadcasted_iota(jnp.int32, y.shape, 1)
        rot = jnp.where(lane % (2 * half) < half, pltpu.roll(y, LANES - half, 1), pltpu.roll(y, half, 1))
    return y * c + rot * s


def _proj_kernel(*refs, half, n_out):
    if half:
        x_ref, w_ref, c_ref, s_ref = refs[:4]
        outs = refs[4:4 + n_out]
    else:
        x_ref, w_ref = refs[:2]
        outs = refs[2:2 + n_out]
    wb_ref = refs[-1]

    @pl.when(pl.program_id(1) == 0)
    def _():
        wb_ref[...] = w_ref[...].astype(BF16)

    x = x_ref[...]
    tn = wb_ref.shape[1]
    ch = _col_chunk(tn)
    for c0 in range(0, tn, ch):
        y = _dot(x, wb_ref[:, c0:c0 + ch])
        if half:
            c = c_ref[...]
            s = s_ref[...]
            for g in range(ch // LANES):
                sl = slice(c0 + g * LANES, c0 + (g + 1) * LANES)
                yg = _rope_group(y[:, g * LANES:(g + 1) * LANES], c, s, half)
                for o in outs:
                    o[:, sl] = yg.astype(o.dtype)
        else:
            for o in outs:
                o[:, c0:c0 + ch] = y.astype(o.dtype)


def _proj(xb, w, layer, col0, ncols, out_dtypes, tm, rope=None):
    m, d = xb.shape
    tn = min(1024, ncols)
    assert ncols % tn == 0 and col0 % tn == 0 and m % tm == 0
    off = col0 // tn
    in_specs = [pl.BlockSpec((tm, d), lambda j, i: (i, 0)),
                pl.BlockSpec((None, d, tn), lambda j, i: (layer, 0, j + off))]
    args = [xb, w]
    half = 0
    if rope is not None:
        ctab, stab, half, tblocks = rope
        in_specs += [pl.BlockSpec((tm, LANES), lambda j, i: (i % tblocks, 0)),
                     pl.BlockSpec((tm, LANES), lambda j, i: (i % tblocks, 0))]
        args += [ctab, stab]
    outs = pl.pallas_call(
        functools.partial(_proj_kernel, half=half, n_out=len(out_dtypes)),
        grid=(ncols // tn, m // tm),
        in_specs=in_specs,
        out_specs=[pl.BlockSpec((tm, tn), lambda j, i: (i, j)) for _ in out_dtypes],
        out_shape=[jax.ShapeDtypeStruct((m, ncols), dt) for dt in out_dtypes],
        scratch_shapes=[pltpu.VMEM((d, tn), BF16)],
        compiler_params=_params("arbitrary", "arbitrary"),
    )(*args)
    return outs


def _mm_res_kernel(a_ref, w_ref, r_ref, o_ref, wb_ref):
    @pl.when(pl.program_id(1) == 0)
    def _():
        wb_ref[...] = w_ref[...].astype(BF16)

    o_ref[...] = r_ref[...] + _dot(a_ref[...], wb_ref[...])


def _mm_res(a, w, layer, res, tm):
    m, k = a.shape
    n = w.shape[2]
    tn = min(512, n)
    return pl.pallas_call(
        _mm_res_kernel,
        grid=(n // tn, m // tm),
        in_specs=[pl.BlockSpec((tm, k), lambda j, i: (i, 0)),
                  pl.BlockSpec((None, k, tn), lambda j, i: (layer, 0, j)),
                  pl.BlockSpec((tm, tn), lambda j, i: (i, j))],
        out_specs=pl.BlockSpec((tm, tn), lambda j, i: (i, j)),
        out_shape=jax.ShapeDtypeStruct((m, n), F32),
        scratch_shapes=[pltpu.VMEM((k, tn), BF16)],
        compiler_params=_params("arbitrary", "arbitrary"),
    )(a, w, res)


def _shifted(u, prev, period):
    row = lax.broadcasted_iota(jnp.int32, u.shape, 0)
    if period == 0:
        p1 = prev[SUBLANES - 1:SUBLANES, :]
        p2 = prev[SUBLANES - 2:SUBLANES - 1, :]
        u1 = jnp.where(row >= 1, pltpu.roll(u, 1, 0), p1)
        u2 = jnp.where(row >= 2, pltpu.roll(u, 2, 0), jnp.where(row == 1, p1, p2))
        return u1, u2
    row = row % period
    u1 = jnp.where(row >= 1, pltpu.roll(u, 1, 0), pltpu.roll(prev, 1, 0))
    u2 = jnp.where(row >= 2, pltpu.roll(u, 2, 0), pltpu.roll(prev, 2, 0))
    return u1, u2


def _dwconv(u, u1, u2, cw):
    return (cw[0:1, :] * u2 + cw[1:2, :] * u1) + cw[2:3, :] * u


def _history(prev_ref, hist_ref, tiles_per_seq):
    if hist_ref is not None:
        return hist_ref[...]

    @pl.when(pl.program_id(1) % tiles_per_seq == 0)
    def _():
        prev_ref[...] = jnp.zeros_like(prev_ref)

    return prev_ref[...]


def _convmix_kernel(*refs, short, tiles_per_seq, period):
    if short:
        x_ref, wb_ref, wc_ref, wh_ref, cw_ref, hist_ref, y_ref, st_ref, wbb, wcb, whb, prev_ref = refs
    else:
        x_ref, wb_ref, wc_ref, wh_ref, cw_ref, y_ref, st_ref, wbb, wcb, whb, prev_ref = refs
        hist_ref = None

    @pl.when(pl.program_id(1) == 0)
    def _():
        wbb[...] = wb_ref[...].astype(BF16)
        wcb[...] = wc_ref[...].astype(BF16)
        whb[...] = wh_ref[...].astype(BF16)

    x = x_ref[...]
    prev_all = _history(prev_ref, hist_ref, tiles_per_seq)
    tn = wbb.shape[1]
    ch = _col_chunk(tn)
    for c0 in range(0, tn, ch):
        sl = slice(c0, c0 + ch)
        u = _dot(x, wcb[:, sl]) * _dot(x, whb[:, sl])
        u1, u2 = _shifted(u, prev_all[:, sl], period)
        conv = _dwconv(u, u1, u2, cw_ref[:, sl])
        y_ref[:, sl] = (_dot(x, wbb[:, sl]) * conv).astype(y_ref.dtype)
        if short:
            st_ref[:, sl] = u
        else:
            prev_ref[:, sl] = u[u.shape[0] - SUBLANES:, :]
            st_ref[:, sl] = u[u.shape[0] - SUBLANES:, :]


def _state_specs(short, tm, tn, tiles_per_seq, col_off=0):
    if short:
        return pl.BlockSpec((tm, tn), lambda j, i: (0, j + col_off))
    return pl.BlockSpec((SUBLANES, tn), lambda j, i: (i // tiles_per_seq, j + col_off))


def _seq_mode(hist_x, m, tm, seq_len):
    if hist_x is not None:
        assert seq_len == SUBLANES and tm == m
        return True, 1, seq_len, m // seq_len, tm
    assert seq_len % tm == 0
    return False, seq_len // tm, 0, m // seq_len, SUBLANES


def _convmix(xb, w_in, conv_w, layer, hist_x, tm, seq_len):
    m, d = xb.shape
    tn = min(512, d)
    nj = d // tn
    short, tiles_per_seq, period, nseq, carry_rows = _seq_mode(hist_x, m, tm, seq_len)
    in_specs = [pl.BlockSpec((tm, d), lambda j, i: (i, 0)),
                pl.BlockSpec((None, d, tn), lambda j, i: (layer, 0, j)),
                pl.BlockSpec((None, d, tn), lambda j, i: (layer, 0, j + nj)),
                pl.BlockSpec((None, d, tn), lambda j, i: (layer, 0, j + 2 * nj)),
                pl.BlockSpec((None, CONV_W, tn), lambda j, i: (layer, 0, j))]
    args = [xb, w_in, w_in, w_in, conv_w]
    if short:
        in_specs.append(pl.BlockSpec((tm, tn), lambda j, i: (0, j)))
        args.append(hist_x)
    y, st = pl.pallas_call(
        functools.partial(_convmix_kernel, short=short, tiles_per_seq=tiles_per_seq, period=period),
        grid=(nj, m // tm),
        in_specs=in_specs,
        out_specs=[pl.BlockSpec((tm, tn), lambda j, i: (i, j)), _state_specs(short, tm, tn, tiles_per_seq)],
        out_shape=[jax.ShapeDtypeStruct((m, d), BF16), jax.ShapeDtypeStruct((nseq * SUBLANES, d), F32)],
        scratch_shapes=[pltpu.VMEM((d, tn), BF16)] * 3 + [pltpu.VMEM((carry_rows, tn), F32)],
        compiler_params=_params("arbitrary", "arbitrary"),
    )(*args)
    return y, st


def _ffn_up_kernel(*refs, short, tiles_per_seq, period):
    if short:
        (x_ref, wg_ref, wu_ref, cg_ref, cu_ref, hg_ref, hu_ref,
         a_ref, sg_ref, su_ref, wgb, wub, pg_ref, pu_ref) = refs
    else:
        x_ref, wg_ref, wu_ref, cg_ref, cu_ref, a_ref, sg_ref, su_ref, wgb, wub, pg_ref, pu_ref = refs
        hg_ref = hu_ref = None

    @pl.when(pl.program_id(1) == 0)
    def _():
        wgb[...] = wg_ref[...].astype(BF16)
        wub[...] = wu_ref[...].astype(BF16)

    prevs = (_history(pg_ref, hg_ref, tiles_per_seq), _history(pu_ref, hu_ref, tiles_per_seq))
    tm = x_ref.shape[0]
    rb = min(tm, ROW_BLOCK)
    tn = wgb.shape[1]
    ch = _col_chunk(tn)
    for c0 in range(0, tn, ch):
        sl = slice(c0, c0 + ch)
        prev = [prevs[0][:, sl], prevs[1][:, sl]]
        for r0 in range(0, tm, rb):
            x = x_ref[r0:r0 + rb, :]
            halves = []
            for h, (w_b, c_ref, s_ref) in enumerate(((wgb, cg_ref, sg_ref), (wub, cu_ref, su_ref))):
                up = _dot(x, w_b[:, sl])
                u1, u2 = _shifted(up, prev[h], period)
                halves.append(_dwconv(up, u1, u2, c_ref[:, sl]))
                if short:
                    s_ref[:, sl] = up
                else:
                    prev[h] = up[rb - SUBLANES:, :]
            g, u = halves
            a_ref[r0:r0 + rb, sl] = ((g * jax.nn.sigmoid(g)) * u).astype(a_ref.dtype)
        if not short:
            for h, (p_ref, s_ref) in enumerate(((pg_ref, sg_ref), (pu_ref, su_ref))):
                p_ref[:, sl] = prev[h]
                s_ref[:, sl] = prev[h]


def _ffn_up(xb, w_up, conv_w, layer, hist_x, tm, seq_len):
    m, d = xb.shape
    d_ff = w_up.shape[2] // 2
    tn = 512 if d_ff % 512 == 0 else LANES
    nj = d_ff // tn
    short, tiles_per_seq, period, nseq, carry_rows = _seq_mode(hist_x, m, tm, seq_len)
    in_specs = [pl.BlockSpec((tm, d), lambda j, i: (i, 0)),
                pl.BlockSpec((None, d, tn), lambda j, i: (layer, 0, j)),
                pl.BlockSpec((None, d, tn), lambda j, i: (layer, 0, j + nj)),
                pl.BlockSpec((None, CONV_W, tn), lambda j, i: (layer, 0, j)),
                pl.BlockSpec((None, CONV_W, tn), lambda j, i: (layer, 0, j + nj))]
    args = [xb, w_up, w_up, conv_w, conv_w]
    if short:
        in_specs += [pl.BlockSpec((tm, tn), lambda j, i: (0, j)), pl.BlockSpec((tm, tn), lambda j, i: (0, j + nj))]
        args += [hist_x, hist_x]
    act, sg, su = pl.pallas_call(
        functools.partial(_ffn_up_kernel, short=short, tiles_per_seq=tiles_per_seq, period=period),
        grid=(nj, m // tm),
        in_specs=in_specs,
        out_specs=[pl.BlockSpec((tm, tn), lambda j, i: (i, j)),
                   _state_specs(short, tm, tn, tiles_per_seq),
                   _state_specs(short, tm, tn, tiles_per_seq)],
        out_shape=[jax.ShapeDtypeStruct((m, d_ff), BF16),
                   jax.ShapeDtypeStruct((nseq * SUBLANES, d_ff), F32),
                   jax.ShapeDtypeStruct((nseq * SUBLANES, d_ff), F32)],
        scratch_shapes=[pltpu.VMEM((d, tn), BF16)] * 2 + [pltpu.VMEM((carry_rows, tn), F32)] * 2,
        compiler_params=_params("arbitrary", "arbitrary"),
    )(*args)
    return act, jnp.concatenate([sg, su], axis=1)


def _log_sigmoid_pair(z):
    l1p = jnp.log(1.0 + jnp.exp(-jnp.abs(z)))
    ls = jnp.minimum(z, 0.0) - l1p
    return ls, ls - z


def _suffix_sum(x, tri):
    hi = x.astype(BF16)
    lo = (x - hi.astype(F32)).astype(BF16)
    return _dot(hi, tri) + _dot(lo, tri)


def _tri(n):
    j = lax.broadcasted_iota(jnp.int32, (n, n), 0)
    s = lax.broadcasted_iota(jnp.int32, (n, n), 1)
    return (j > s).astype(BF16)


def _sb_prompt_kernel(q_ref, k_ref, v_ref, tri_ref, o_ref, *, tq, scale):
    qi = pl.program_id(2)
    q = q_ref[...]
    tri = tri_ref[...]

    def block(ki, carry, acc, diag):
        start = pl.multiple_of(ki * tq, tq)
        k = k_ref[pl.ds(start, tq), :]
        v = v_ref[pl.ds(start, tq), :]
        z = _dot_t(q, k) * scale
        ls, l1m = _log_sigmoid_pair(z)
        if diag:
            row = lax.broadcasted_iota(jnp.int32, z.shape, 0)
            col = lax.broadcasted_iota(jnp.int32, z.shape, 1)
            valid = col < row
            l1m = jnp.where(valid, l1m, 0.0)
        after = _suffix_sum(l1m, tri)
        a = jnp.exp(ls + after + carry)
        if diag:
            a = jnp.where(valid, a, 0.0)
        acc = acc + _dot(a.astype(BF16), v)
        carry = carry + jnp.sum(l1m, axis=1, keepdims=True)
        return carry, acc

    carry0 = jnp.zeros((tq, 1), F32)
    acc0 = jnp.zeros((tq, v_ref.shape[1]), F32)
    carry, acc = block(qi, carry0, acc0, True)

    def more(c):
        return (c[0] < qi) & (jnp.max(c[1]) > SB_SKIP_BELOW)

    def body(c):
        carry, acc = block(qi - 1 - c[0], c[1], c[2], False)
        return c[0] + 1, carry, acc

    _, carry, acc = lax.while_loop(more, body, (jnp.int32(0), carry, acc))
    o_ref[...] = acc.astype(o_ref.dtype)


def _sb_prompt(qb, kb, vb, batch, seq, n_heads, hd):
    m, d = qb.shape
    tq = min(256, seq)
    nq = seq // tq
    return pl.pallas_call(
        functools.partial(_sb_prompt_kernel, tq=tq, scale=hd ** -0.5),
        grid=(batch, n_heads, nq),
        in_specs=[pl.BlockSpec((tq, hd), lambda b, h, q: (b * nq + q, h)),
                  pl.BlockSpec((seq, hd), lambda b, h, q: (b, h)),
                  pl.BlockSpec((seq, hd), lambda b, h, q: (b, h)),
                  pl.BlockSpec((tq, tq), lambda b, h, q: (0, 0))],
        out_specs=pl.BlockSpec((tq, hd), lambda b, h, q: (b * nq + q, h)),
        out_shape=jax.ShapeDtypeStruct((m, d), BF16),
        compiler_params=_params("arbitrary", "arbitrary", "arbitrary"),
    )(qb, kb, vb, _tri(tq))


def _online_softmax_step(s, mask, m, l, acc, v):
    if mask is not None:
        s = jnp.where(mask, s, NEG_BIG)
    m_new = jnp.maximum(m, jnp.max(s, axis=1, keepdims=True))
    p = jnp.exp(s - m_new)
    if mask is not None:
        p = jnp.where(mask, p, 0.0)
    alpha = jnp.exp(m - m_new)
    l = alpha * l + jnp.sum(p, axis=1, keepdims=True)
    acc = alpha * acc + _dot(p.astype(BF16), v)
    return m_new, l, acc


def _lambda_full(lam_ref, lambda_init):
    lf = lam_ref[...]
    s01 = jnp.sum(lf[0:1, :] * lf[1:2, :], axis=1, keepdims=True)
    s23 = jnp.sum(lf[2:3, :] * lf[3:4, :], axis=1, keepdims=True)
    return jnp.exp(s01) - jnp.exp(s23) + lambda_init


def _sub_norm(o, g, lambda_init):
    ms = jnp.mean(o * o, axis=-1, keepdims=True)
    return ((o * lax.rsqrt(ms + NORM_EPS)) * g) * (1.0 - lambda_init)


def _diff_prompt_kernel(q_ref, k_ref, v_ref, lam_ref, g_ref, o_ref, *, tq, hd, scale, lambda_init):
    qi = pl.program_id(2)
    dv = v_ref.shape[1]

    def block(ki, state, diag):
        start = pl.multiple_of(ki * tq, tq)
        v = v_ref[pl.ds(start, tq), :]
        mask = None
        if diag:
            row = lax.broadcasted_iota(jnp.int32, (tq, tq), 0)
            col = lax.broadcasted_iota(jnp.int32, (tq, tq), 1)
            mask = col <= row
        new = []
        for i in range(2):
            q = q_ref[:, i * hd:(i + 1) * hd]
            k = k_ref[pl.ds(start, tq), i * hd:(i + 1) * hd]
            s = _dot_t(q, k) * scale
            new.append(_online_softmax_step(s, mask, *state[i], v))
        return tuple(new)

    init = tuple((jnp.full((tq, 1), NEG_BIG, F32), jnp.zeros((tq, 1), F32), jnp.zeros((tq, dv), F32))
                 for _ in range(2))
    state = lax.fori_loop(0, qi, lambda ki, st: block(ki, st, False), init)
    state = block(qi, state, True)
    lam = _lambda_full(lam_ref, lambda_init)
    (_, l0, a0), (_, l1, a1) = state
    o = a0 / l0 - lam * (a1 / l1)
    o_ref[...] = _sub_norm(o, g_ref[...], lambda_init).astype(o_ref.dtype)


def _diff_prompt(qb, kb, vb, lam, g_sub, batch, seq, n_dh, hd, lambda_init):
    m, d = qb.shape
    tq = min(FLASH_BLOCK, seq)
    nq = seq // tq
    w = 2 * hd
    return pl.pallas_call(
        functools.partial(_diff_prompt_kernel, tq=tq, hd=hd, scale=hd ** -0.5, lambda_init=lambda_init),
        grid=(batch, n_dh, nq),
        in_specs=[pl.BlockSpec((tq, w), lambda b, h, q: (b * nq + q, h)),
                  pl.BlockSpec((seq, w), lambda b, h, q: (b, h)),
                  pl.BlockSpec((seq, w), lambda b, h, q: (b, h)),
                  pl.BlockSpec(lam.shape, lambda b, h, q: (0, 0)),
                  pl.BlockSpec((1, w), lambda b, h, q: (0, 0))],
        out_specs=pl.BlockSpec((tq, w), lambda b, h, q: (b * nq + q, h)),
        out_shape=jax.ShapeDtypeStruct((m, d), BF16),
        compiler_params=_params("arbitrary", "arbitrary", "arbitrary"),
    )(qb, kb, vb, lam, g_sub.reshape(1, w))


def _dsa_prompt_kernel(q_ref, k_ref, v_ref, sel_ref, o_ref, *, tq, scale):
    qi = pl.program_id(2)
    q = q_ref[...]
    dv = v_ref.shape[1]

    def block(ki, state):
        start = pl.multiple_of(ki * tq, tq)
        k = k_ref[pl.ds(start, tq), :]
        v = v_ref[pl.ds(start, tq), :]
        s = _dot_t(q, k) * scale + sel_ref[:, pl.ds(start, tq)].astype(F32)
        return _online_softmax_step(s, None, *state, v)

    init = (jnp.full((tq, 1), M_INIT, F32), jnp.zeros((tq, 1), F32), jnp.zeros((tq, dv), F32))
    _, l, acc = lax.fori_loop(0, qi + 1, block, init)
    o_ref[...] = (acc / l).astype(o_ref.dtype)


def _dsa_prompt(qb, kb, vb, sel, batch, seq, n_heads, hd):
    m, d = qb.shape
    tq = min(FLASH_BLOCK, seq)
    nq = seq // tq
    return pl.pallas_call(
        functools.partial(_dsa_prompt_kernel, tq=tq, scale=hd ** -0.5),
        grid=(batch, n_heads, nq),
        in_specs=[pl.BlockSpec((tq, hd), lambda b, h, q: (b * nq + q, h)),
                  pl.BlockSpec((seq, hd), lambda b, h, q: (b, h)),
                  pl.BlockSpec((seq, hd), lambda b, h, q: (b, h)),
                  pl.BlockSpec((tq, seq), lambda b, h, q: (b * nq + q, 0))],
        out_specs=pl.BlockSpec((tq, hd), lambda b, h, q: (b * nq + q, h)),
        out_shape=jax.ShapeDtypeStruct((m, d), BF16),
        compiler_params=_params("arbitrary", "arbitrary", "arbitrary"),
    )(qb, kb, vb, sel)


def _order_key(x):
    bits = pltpu.bitcast(x + 0.0, jnp.int32)
    return jnp.where(bits >= 0, bits, bits ^ jnp.int32(0x7FFFFFFF))


def _kth_largest_key(count_ge, rows, k):
    zero = jnp.zeros((rows, 1), jnp.int32)
    t0 = jnp.where(count_ge(zero) >= k, zero, jnp.int32(INT_MIN))

    def body(it, t):
        cand = t | jnp.left_shift(jnp.int32(1), jnp.int32(30) - it)
        return jnp.where(count_ge(cand) >= k, cand, t)

    return lax.fori_loop(0, 31, body, t0)


def _store_topk_mask(o_ref, key, valid, k, t):
    ge = valid & (key >= t)
    n_ge = jnp.sum(ge.astype(jnp.int32), axis=1, keepdims=True)
    has_ties = jnp.max(n_ge) > k

    @pl.when(jnp.logical_not(has_ties))
    def _():
        o_ref[...] = jnp.where(ge, 0.0, NEG_BIG).astype(o_ref.dtype)

    @pl.when(has_ties)
    def _():
        gt = key > t
        eq = key == t
        need = k - jnp.sum(gt.astype(jnp.int32), axis=1, keepdims=True)
        col = lax.broadcasted_iota(jnp.int32, key.shape, 1)
        nbits = int(key.shape[1]).bit_length()

        def body(it, j):
            cand = j | jnp.left_shift(jnp.int32(1), jnp.int32(nbits - 1) - it)
            c = jnp.sum((eq & (col < cand)).astype(jnp.int32), axis=1, keepdims=True)
            return jnp.where(c < need, cand, j)

        j = lax.fori_loop(0, nbits, body, jnp.zeros_like(t))
        sel = valid & (gt | (eq & (col <= j)))
        o_ref[...] = jnp.where(sel, 0.0, NEG_BIG).astype(o_ref.dtype)


def _select_prompt_kernel(iq_ref, ik_ref, iw_ref, o_ref, key_ref, qh_ref, *, tq, cw, n_ih, idim, k_top, wscale):
    qi = pl.program_id(1)
    seq = key_ref.shape[1]
    n_need = (qi * tq) // cw + 1
    iw = iw_ref[...] * wscale
    lane = lax.broadcasted_iota(jnp.int32, (tq, LANES), 1)
    per_group = LANES // idim
    for h in range(n_ih):
        g, r = divmod(h, per_group)
        qg = iq_ref[:, g * LANES:(g + 1) * LANES].astype(F32)
        qh_ref[h] = jnp.where((lane >= r * idim) & (lane < (r + 1) * idim), qg, 0.0).astype(BF16)

    def chunk_at(c):
        return pl.ds(pl.multiple_of(c * cw, cw), cw)

    def score_chunk(c, carry):
        ikc = ik_ref[chunk_at(c), :]
        acc = None
        for h in range(n_ih):
            s = jnp.maximum(_dot_t(qh_ref[h], ikc), 0.0) * iw[:, idim + h:idim + h + 1]
            acc = s if acc is None else acc + s
        row = qi * tq + lax.broadcasted_iota(jnp.int32, (tq, cw), 0)
        col = c * cw + lax.broadcasted_iota(jnp.int32, (tq, cw), 1)
        key_ref[:, chunk_at(c)] = _order_key(jnp.where(col <= row, acc, -jnp.inf))
        return carry

    def fill_chunk(c, carry):
        key_ref[:, chunk_at(c)] = jnp.full((tq, cw), INT_MIN, jnp.int32)
        return carry

    lax.fori_loop(0, n_need, score_chunk, 0)
    lax.fori_loop(n_need, seq // cw, fill_chunk, 0)

    def count_ge(t):
        def body(c, acc):
            m = (key_ref[:, chunk_at(c)] >= t).astype(jnp.int32)
            for v in range(cw // LANES):
                acc = acc + m[:, v * LANES:(v + 1) * LANES]
            return acc

        acc = lax.fori_loop(0, n_need, body, jnp.zeros((tq, LANES), jnp.int32))
        return jnp.sum(acc, axis=1, keepdims=True)

    t = _kth_largest_key(count_ge, tq, k_top)
    row = qi * tq + lax.broadcasted_iota(jnp.int32, (tq, seq), 0)
    col = lax.broadcasted_iota(jnp.int32, (tq, seq), 1)
    _store_topk_mask(o_ref, key_ref[...], col <= row, k_top, t)


def _select_prompt(iqb, ik2b, tail, batch, seq, n_ih, idim, k_top):
    m = iqb.shape[0]
    tq = min(128, seq)
    cw = min(512, seq)
    nq = seq // tq
    assert LANES % idim == 0 and cw % tq == 0 and seq % cw == 0
    return pl.pallas_call(
        functools.partial(_select_prompt_kernel, tq=tq, cw=cw, n_ih=n_ih, idim=idim, k_top=k_top,
                          wscale=(idim ** -0.5) * (n_ih ** -0.5)),
        grid=(batch, nq),
        in_specs=[pl.BlockSpec((tq, n_ih * idim), lambda b, q: (b * nq + q, 0)),
                  pl.BlockSpec((seq, LANES), lambda b, q: (b, 0)),
                  pl.BlockSpec((tq, LANES), lambda b, q: (b * nq + q, 0))],
        out_specs=pl.BlockSpec((tq, seq), lambda b, q: (b * nq + q, 0)),
        out_shape=jax.ShapeDtypeStruct((m, seq), BF16),
        scratch_shapes=[pltpu.VMEM((tq, seq), jnp.int32), pltpu.VMEM((n_ih, tq, LANES), BF16)],
        compiler_params=_params("arbitrary", "arbitrary"),
    )(iqb, ik2b, tail)


def _sel_sample_kernel(pt_ref, qm_ref, w_ref, iknew_ref, *refs, pages_per_step, n_pages, ds, k_top):
    ik_refs = refs[:pages_per_step]
    o_ref, sc_ref = refs[pages_per_step:pages_per_step + 2]
    s_id = pl.program_id(1)
    qm = qm_ref[...]
    w = w_ref[...]
    page = ik_refs[0].shape[0]

    def scores(ik):
        s = jnp.maximum(_dot_t(qm, ik.astype(BF16)), 0.0) * w
        tot = s[0:ds, :]
        for h in range(1, s.shape[0] // ds):
            tot = tot + s[h * ds:(h + 1) * ds, :]
        return tot

    for c in range(pages_per_step):
        pg = s_id * pages_per_step + c
        sc_ref[:, pl.ds(pl.multiple_of(pg * page, page), page)] = scores(ik_refs[c][...])

    @pl.when(s_id == pl.num_programs(1) - 1)
    def _():
        t = lax.broadcasted_iota(jnp.int32, (ds, page), 0)
        j = lax.broadcasted_iota(jnp.int32, (ds, page), 1)
        sc_ref[:, n_pages * page:] = jnp.where(j <= t, scores(iknew_ref[...]), -jnp.inf)
        col = lax.broadcasted_iota(jnp.int32, sc_ref.shape, 1)
        row = lax.broadcasted_iota(jnp.int32, sc_ref.shape, 0)
        key = _order_key(sc_ref[...])

        def count_ge(thr):
            return jnp.sum((key >= thr).astype(jnp.int32), axis=1, keepdims=True)

        thr = _kth_largest_key(count_ge, ds, k_top)
        _store_topk_mask(o_ref, key, col <= n_pages * page + row, k_top, thr)


def _select_sample(page_table, qm, wcol, iknew, pool, layer, n_pages, k_top):
    db, rows, idim = qm.shape
    page = pool.shape[2]
    ds = SUBLANES
    pps = 8 if n_pages % 8 == 0 else (4 if n_pages % 4 == 0 else 1)
    n_steps = n_pages // pps
    nk = (n_pages + 1) * page

    def pool_spec(c):
        return pl.BlockSpec((None, None, page, idim), lambda b, s, pt: (layer, pt[b, s * pps + c], 0, 0))

    grid_spec = pltpu.PrefetchScalarGridSpec(
        num_scalar_prefetch=1,
        grid=(db, n_steps),
        in_specs=[pl.BlockSpec((None, rows, idim), lambda b, s, pt: (b, 0, 0)),
                  pl.BlockSpec((None, rows, 1), lambda b, s, pt: (b, 0, 0)),
                  pl.BlockSpec((None, page, idim), lambda b, s, pt: (b, 0, 0))]
                 + [pool_spec(c) for c in range(pps)],
        out_specs=pl.BlockSpec((None, ds, nk), lambda b, s, pt: (b, 0, 0)),
        scratch_shapes=[pltpu.VMEM((ds, nk), F32)],
    )
    return pl.pallas_call(
        functools.partial(_sel_sample_kernel, pages_per_step=pps, n_pages=n_pages, ds=ds, k_top=k_top),
        grid_spec=grid_spec,
        out_shape=jax.ShapeDtypeStruct((db, ds, nk), BF16),
        compiler_params=_params("arbitrary", "arbitrary"),
    )(page_table, qm, wcol, iknew, *([pool] * pps))


def _class_reduce(x, n_cls, op):
    return _class_finish(_lane_fold(x, op), n_cls, op)


def _lane_fold(x, op):
    r = x[:, 0:LANES]
    for v in range(1, x.shape[1] // LANES):
        r = op(r, x[:, v * LANES:(v + 1) * LANES])
    return r


def _class_finish(r, n_cls, op):
    sh = LANES // 2
    while sh >= n_cls:
        r = op(r, pltpu.roll(r, sh, 1))
        sh //= 2
    return r


def _tile_lanes(r, ncol):
    return jnp.concatenate([r] * (ncol // LANES), axis=1)


def _class_column(r, n_cls):
    return jnp.concatenate([r[:, c:c + 1] for c in range(n_cls)], axis=0)


def _suffix_by_class(x, n_cls):
    lane = lax.broadcasted_iota(jnp.int32, (x.shape[0], LANES), 1)
    nv = x.shape[1] // LANES
    excl_in, tots = [], []
    for v in range(nv):
        xv = x[:, v * LANES:(v + 1) * LANES]
        inc = xv
        sh = n_cls
        while sh < LANES:
            inc = inc + jnp.where(lane < LANES - sh, pltpu.roll(inc, LANES - sh, 1), 0.0)
            sh *= 2
        tot = xv
        sh = LANES // 2
        while sh >= n_cls:
            tot = tot + pltpu.roll(tot, sh, 1)
            sh //= 2
        excl_in.append(inc - xv)
        tots.append(tot)
    later = jnp.zeros((x.shape[0], LANES), F32)
    out = [None] * nv
    for v in reversed(range(nv)):
        out[v] = excl_in[v] + later
        later = later + tots[v]
    return jnp.concatenate(out, axis=1), later


def _paged_kernel(pt_ref, *refs, kind, pages_per_step, n_cls, n_grp, scale, lambda_init):
    n_in = 3 + (2 if kind == "dsa" else 0) + 2 * pages_per_step + (2 if kind == "diff" else 0)
    ins, o_ref, (m_ref, l_ref, acc_ref) = refs[:n_in], refs[n_in], refs[n_in + 1:]
    q_ref, knew_ref, vnew_ref = ins[:3]
    pos = 3
    if kind == "dsa":
        selnew_ref, sel_ref = ins[pos:pos + 2]
        pos += 2
    k_refs = ins[pos:pos + pages_per_step]
    v_refs = ins[pos + pages_per_step:pos + 2 * pages_per_step]
    pos += 2 * pages_per_step
    if kind == "diff":
        lam_ref, g_ref = ins[pos:pos + 2]
    s_id = pl.program_id(1)
    page = knew_ref.shape[0]
    ncol = page * n_cls
    col = lax.broadcasted_iota(jnp.int32, (SUBLANES, ncol), 1)
    cls = col % n_cls

    def fold(s_big):
        out = s_big[0:SUBLANES, :]
        for c in range(1, n_cls):
            out = jnp.where(cls == c, s_big[c * SUBLANES:(c + 1) * SUBLANES, :], out)
        return out

    def unfold(p):
        return jnp.concatenate([jnp.where(cls == c, p, 0.0) for c in range(n_cls)], axis=0).astype(BF16)

    def widen(x):
        lane = lax.broadcasted_iota(jnp.int32, (SUBLANES, LANES), 1)
        per = LANES // n_cls
        return jnp.concatenate([jnp.take_along_axis(x, v * per + lane // n_cls, axis=1)
                                for v in range(ncol // LANES)], axis=1)

    def process(pages):
        v2s = [v_ref[...].reshape(ncol, v_ref.shape[-1]).astype(BF16) for _, v_ref, _, _ in pages]
        weights, alphas = [], []
        for g in range(n_grp):
            ss = []
            for k_ref, _, mask, bias in pages:
                k = k_ref[...] if n_grp == 1 else k_ref[:, pl.ds(g, n_cls, stride=n_grp), :]
                s = fold(_dot_t(q_ref[g], k.reshape(ncol, k.shape[-1]).astype(BF16)) * scale)
                if bias is not None:
                    s = s + bias
                if mask is not None and kind != "sb":
                    s = jnp.where(mask, s, NEG_BIG)
                ss.append(s)
            ws = []
            if kind == "sb":
                carry = m_ref[g]
                for (_, _, mask, _), s in zip(pages, ss):
                    ls, l1m = _log_sigmoid_pair(s)
                    if mask is not None:
                        l1m = jnp.where(mask, l1m, 0.0)
                    excl, total = _suffix_by_class(l1m, n_cls)
                    a = jnp.exp(ls + excl + _tile_lanes(carry, ncol))
                    ws.append(a if mask is None else jnp.where(mask, a, 0.0))
                    carry = carry + total
                m_ref[g] = carry
            else:
                m_old = m_ref[g]
                m_blk = None
                for s in ss:
                    r = _lane_fold(s, jnp.maximum)
                    m_blk = r if m_blk is None else jnp.maximum(m_blk, r)
                m_new = jnp.maximum(m_old, _class_finish(m_blk, n_cls, jnp.maximum))
                mt = _tile_lanes(m_new, ncol)
                l_blk = None
                for (_, _, mask, _), s in zip(pages, ss):
                    p = jnp.exp(s - mt)
                    if mask is not None:
                        p = jnp.where(mask, p, 0.0)
                    r = _lane_fold(p, jnp.add)
                    l_blk = r if l_blk is None else l_blk + r
                    ws.append(p)
                alpha = jnp.exp(m_old - m_new)
                l_ref[g] = alpha * l_ref[g] + _class_finish(l_blk, n_cls, jnp.add)
                m_ref[g] = m_new
                alphas.append(_class_column(alpha, n_cls))
            weights.append(ws)
        acc_blk = None
        for c, v2 in enumerate(v2s):
            big = [unfold(weights[g][c]) for g in range(n_grp)]
            part = _dot(big[0] if n_grp == 1 else jnp.concatenate(big, axis=0), v2)
            acc_blk = part if acc_blk is None else acc_blk + part
        if kind == "sb":
            acc_ref[...] += acc_blk
        else:
            alpha_col = alphas[0] if n_grp == 1 else jnp.concatenate(alphas, axis=0)
            acc_ref[...] = acc_ref[...] * alpha_col + acc_blk

    @pl.when(s_id == 0)
    def _():
        acc_ref[...] = jnp.zeros_like(acc_ref)
        l_ref[...] = jnp.zeros_like(l_ref)
        m_ref[...] = jnp.full_like(m_ref, {"sb": 0.0, "diff": NEG_BIG, "dsa": M_INIT}[kind])
        t = lax.broadcasted_iota(jnp.int32, (SUBLANES, ncol), 0)
        slot = col // n_cls
        if kind == "sb":
            process([(knew_ref, vnew_ref, slot < t, None)])
        elif kind == "diff":
            process([(knew_ref, vnew_ref, slot <= t, None)])
        else:
            process([(knew_ref, vnew_ref, None, widen(selnew_ref[...].astype(F32)))])

    pages = []
    for c in range(pages_per_step):
        bias = None
        if kind == "dsa":
            lo = (pages_per_step - 1 - c) * page
            bias = widen(sel_ref[:, lo:lo + page].astype(F32))
        pages.append((k_refs[c], v_refs[c], None, bias))
    process(pages)

    @pl.when(s_id == pl.num_programs(1) - 1)
    def _():
        if kind == "sb":
            o_ref[...] = acc_ref[...].astype(o_ref.dtype)
        elif kind == "dsa":
            o_ref[...] = (acc_ref[...] / _class_column(l_ref[0], n_cls)).astype(o_ref.dtype)
        else:
            lam = _lambda_full(lam_ref, lambda_init)
            rows = n_cls * SUBLANES
            a0 = acc_ref[0:rows, :] / _class_column(l_ref[0], n_cls)
            a1 = acc_ref[rows:2 * rows, :] / _class_column(l_ref[1], n_cls)
            o_ref[...] = _sub_norm(a0 - lam * a1, g_ref[...], lambda_init).astype(o_ref.dtype)


def _paged_attention(kind, page_table, layer, q, knew, vnew, kpool, vpool, n_pages, *,
                     sel=None, lam=None, g_sub=None, lambda_init=0.0):
    db, n_grp, rows, hd = q.shape
    n_cls = rows // SUBLANES
    page = kpool.shape[2]
    dv = vpool.shape[4]
    ncol = page * n_cls
    pps = 8 if n_pages % 8 == 0 else (4 if n_pages % 4 == 0 else 1)
    n_steps = n_pages // pps

    def pool_spec(pool, c):
        return pl.BlockSpec((None, None) + pool.shape[2:],
                            lambda b, s, pt: (layer, pt[b, n_pages - 1 - (s * pps + c)], 0, 0, 0))

    in_specs = [pl.BlockSpec((None,) + q.shape[1:], lambda b, s, pt: (b, 0, 0, 0)),
                pl.BlockSpec((None,) + knew.shape[1:], lambda b, s, pt: (b, 0, 0, 0)),
                pl.BlockSpec((None,) + vnew.shape[1:], lambda b, s, pt: (b, 0, 0, 0))]
    args = [q, knew, vnew]
    if kind == "dsa":
        in_specs += [pl.BlockSpec((None, SUBLANES, page), lambda b, s, pt: (b, 0, n_pages)),
                     pl.BlockSpec((None, SUBLANES, pps * page), lambda b, s, pt: (b, 0, n_steps - 1 - s))]
        args += [sel, sel]
    in_specs += [pool_spec(kpool, c) for c in range(pps)] + [pool_spec(vpool, c) for c in range(pps)]
    args += [kpool] * pps + [vpool] * pps
    if kind == "diff":
        in_specs += [pl.BlockSpec(lam.shape, lambda b, s, pt: (0, 0)),
                     pl.BlockSpec((1, dv), lambda b, s, pt: (0, 0))]
        args += [lam, g_sub.reshape(1, dv)]
    grid_spec = pltpu.PrefetchScalarGridSpec(
        num_scalar_prefetch=1,
        grid=(db, n_steps),
        in_specs=in_specs,
        out_specs=pl.BlockSpec((None, rows, dv), lambda b, s, pt: (b, 0, 0)),
        scratch_shapes=[pltpu.VMEM((n_grp, SUBLANES, LANES), F32), pltpu.VMEM((n_grp, SUBLANES, LANES), F32),
                        pltpu.VMEM((n_grp * rows, dv), F32)],
    )
    return pl.pallas_call(
        functools.partial(_paged_kernel, kind=kind, pages_per_step=pps, n_cls=n_cls, n_grp=n_grp,
                          scale=hd ** -0.5, lambda_init=lambda_init),
        grid_spec=grid_spec,
        out_shape=jax.ShapeDtypeStruct((db, rows, dv), BF16),
        compiler_params=_params("arbitrary", "arbitrary"),
    )(page_table, *args)


def _rope_tables(pos, half, ident_from=None):
    inv = ROPE_THETA ** (-jnp.arange(half, dtype=F32) / half)
    ang = pos.astype(F32)[:, None] * inv
    cos, sin = jnp.cos(ang), jnp.sin(ang)
    reps = LANES // (2 * half)
    c = jnp.tile(jnp.concatenate([cos, cos], axis=1), (1, reps))
    s = jnp.tile(jnp.concatenate([-sin, sin], axis=1), (1, reps))
    if ident_from is not None:
        lane = jnp.arange(LANES)[None, :]
        c = jnp.where(lane < ident_from, c, 1.0)
        s = jnp.where(lane < ident_from, s, 0.0)
    return c, s


def _expand_history(hist):
    nseq, _, c = hist.shape
    z = jnp.zeros((nseq, SUBLANES - 2, c), hist.dtype)
    blocks = jnp.concatenate([z, hist], axis=1).reshape(nseq * SUBLANES, c)
    return jnp.roll(blocks, -SUBLANES, axis=0)


def _last_two(st, nseq):
    return st.reshape(nseq, SUBLANES, st.shape[1])[:, SUBLANES - 2:, :]


def _pad_rows(x, rows):
    return jnp.pad(x, ((0, 0), (0, rows - x.shape[1])) + ((0, 0),) * (x.ndim - 2))


def _rows_by_class(x, db, ds, n_grp, n_cls):
    hd = x.shape[1] // (n_cls * n_grp)
    return x.reshape(db, ds, n_cls, n_grp, hd).transpose(0, 3, 2, 1, 4).reshape(db, n_grp, n_cls * ds, hd)


def _rows_by_query(y, db, ds):
    n_cls, dv = y.shape[1] // ds, y.shape[2]
    return y.reshape(db, n_cls, ds, dv).transpose(0, 2, 1, 3).reshape(db * ds, n_cls * dv)


def kernel(x_prompt, x_sample, state_conv_mix, cache_sb_k, cache_sb_v, cache_diff_k, cache_diff_v, cache_dsa_k, cache_dsa_v, cache_dsa_idx_k, state_ffn_conv, page_table, g_mix, g_ffn, g_final, w_a_in, conv_a, w_a_out, w_b_qkv, w_b_out, w_c_qkv, lam_c, g_c_subln, w_c_out, w_d_in, w_d_out, w_ffn_up, conv_ffn, w_ffn_down):
    batch, seq, d = x_prompt.shape
    db, ds, _ = x_sample.shape
    depth = g_mix.shape[0]
    n_pool, page = cache_sb_k.shape[1], cache_sb_k.shape[2]
    n_heads, hd = cache_sb_k.shape[3], cache_sb_k.shape[4]
    n_dh = cache_diff_k.shape[3]
    idim = cache_dsa_idx_k.shape[3]
    n_pages = page_table.shape[1]
    past = n_pages * page
    n_ih = (w_d_in.shape[2] - 3 * d - idim) // (idim + 1)
    assert ds == SUBLANES and hd == LANES and cache_diff_k.shape[5] == hd and n_dh * 2 == n_heads
    assert (n_ih * idim) % LANES == 0 and idim + n_ih <= LANES

    mp, ms = batch * seq, db * ds
    tm_p = min(512, seq)
    streams = (("p", tm_p, seq), ("s", ms, ds))
    x = {"p": x_prompt.reshape(mp, d), "s": x_sample.reshape(ms, d)}
    pos = {"p": jnp.arange(seq), "s": past + jnp.arange(ds)}
    pos_rows = {"p": pos["p"], "s": jnp.tile(pos["s"], db)}
    rope_blocks = {"p": seq // tm_p, "s": 1}
    rope_hd = {n: _rope_tables(pos_rows[n], hd // 2) for n in x}
    rope_idx = {n: _rope_tables(pos_rows[n], idim // 2) for n in x}
    rope_tail = {n: _rope_tables(pos_rows[n], idim // 2, ident_from=idim) for n in x}
    k_top = {"p": min(TOPK_MAX, seq // 4), "s": min(TOPK_MAX, (past + ds) // 4)}
    nseq = {"p": batch, "s": db}

    outs = {name: {"p": [], "s": []} for name in
            ("conv", "sbk", "sbv", "dk", "dv", "ak", "av", "ai", "fc")}

    def shaped(a, n, tail):
        return a.reshape((nseq[n], seq if n == "p" else ds) + tail)

    for i in range(depth):
        kind, j = i % 4, i // 4
        for n, tm, slen in streams:
            xb = _rmsnorm(x[n], g_mix[i], BF16, tm)
            if kind == 0:
                hist = None if n == "p" else _expand_history(state_conv_mix[j])
                y, st = _convmix(xb, w_a_in, conv_a, j, hist, tm, slen)
                outs["conv"][n].append(_last_two(st, nseq[n]))
                w_out = w_a_out
            elif kind == 1:
                w = w_b_qkv
                (qb,) = _proj(xb, w, j, 0, d, (BF16,), tm)
                kf, kb = _proj(xb, w, j, d, d, (F32, BF16), tm)
                vf, vb = _proj(xb, w, j, 2 * d, d, (F32, BF16), tm)
                outs["sbk"][n].append(shaped(kf, n, (n_heads, hd)))
                outs["sbv"][n].append(shaped(vf, n, (n_heads, hd)))
                if n == "p":
                    y = _sb_prompt(qb, kb, vb, batch, seq, n_heads, hd)
                else:
                    y = _rows_by_query(_paged_attention(
                        "sb", page_table, j, _rows_by_class(qb, db, ds, 1, n_heads),
                        _pad_rows(kf.reshape(db, ds, n_heads, hd), page),
                        _pad_rows(vf.reshape(db, ds, n_heads, hd), page),
                        cache_sb_k, cache_sb_v, n_pages), db, ds)
                w_out = w_b_out
            elif kind == 2:
                lambda_init = 0.8 - 0.6 * math.exp(-0.3 * i)
                w = w_c_qkv
                rope = rope_hd[n] + (hd // 2, rope_blocks[n])
                (qb,) = _proj(xb, w, j, 0, d, (BF16,), tm, rope)
                kf, kb = _proj(xb, w, j, d, d, (F32, BF16), tm, rope)
                vf, vb = _proj(xb, w, j, 2 * d, d, (F32, BF16), tm)
                outs["dk"][n].append(shaped(kf, n, (n_dh, 2, hd)))
                outs["dv"][n].append(shaped(vf, n, (n_dh, 2 * hd)))
                if n == "p":
                    y = _diff_prompt(qb, kb, vb, lam_c[j], g_c_subln[j], batch, seq, n_dh, hd, lambda_init)
                else:
                    y = _rows_by_query(_paged_attention(
                        "diff", page_table, j, _rows_by_class(qb, db, ds, 2, n_dh),
                        _pad_rows(kf.reshape(db, ds, n_heads, hd), page),
                        _pad_rows(vf.reshape(db, ds, n_dh, 2 * hd), page),
                        cache_diff_k.reshape(cache_diff_k.shape[:3] + (n_heads, hd)), cache_diff_v, n_pages,
                        lam=lam_c[j], g_sub=g_c_subln[j], lambda_init=lambda_init), db, ds)
                w_out = w_c_out
            else:
                w = w_d_in
                rope = rope_hd[n] + (hd // 2, rope_blocks[n])
                (qb,) = _proj(xb, w, j, 0, d, (BF16,), tm, rope)
                kf, kb = _proj(xb, w, j, d, d, (F32, BF16), tm, rope)
                vf, vb = _proj(xb, w, j, 2 * d, d, (F32, BF16), tm)
                (iqb,) = _proj(xb, w, j, 3 * d, n_ih * idim, (BF16,), tm, rope_idx[n] + (idim // 2, rope_blocks[n]))
                o2 = 3 * d + n_ih * idim
                w_ik, w_iw = w[j:j + 1, :, o2:o2 + idim], w[j:j + 1, :, o2 + idim:]
                w_tail = jnp.concatenate([w_ik, w_iw, jnp.zeros((1, d, LANES - idim - n_ih), F32)], axis=2)
                (tail,) = _proj(xb, w_tail, 0, 0, LANES, (F32,), tm, rope_tail[n] + (idim // 2, rope_blocks[n]))
                outs["ak"][n].append(shaped(kf, n, (n_heads, hd)))
                outs["av"][n].append(shaped(vf, n, (n_heads, hd)))
                outs["ai"][n].append(shaped(tail[:, :idim], n, (idim,)))
                if n == "p":
                    w_ik2 = jnp.concatenate([w_ik] * (LANES // idim), axis=2)
                    (ik2b,) = _proj(xb, w_ik2, 0, 0, LANES, (BF16,), tm, rope_idx[n] + (idim // 2, rope_blocks[n]))
                    sel = _select_prompt(iqb, ik2b, tail, batch, seq, n_ih, idim, k_top[n])
                    y = _dsa_prompt(qb, kb, vb, sel, batch, seq, n_heads, hd)
                else:
                    qm = iqb.reshape(db, ds, n_ih, idim).transpose(0, 2, 1, 3).reshape(db, n_ih * ds, idim)
                    wcol = tail[:, idim:idim + n_ih].reshape(db, ds, n_ih).transpose(0, 2, 1).reshape(db, n_ih * ds, 1)
                    wcol = wcol * ((idim ** -0.5) * (n_ih ** -0.5))
                    iknew = _pad_rows(tail[:, :idim].reshape(db, ds, idim), page)
                    sel = _select_sample(page_table, qm, wcol, iknew, cache_dsa_idx_k, j, n_pages, k_top[n])
                    y = _rows_by_query(_paged_attention(
                        "dsa", page_table, j, _rows_by_class(qb, db, ds, 1, n_heads),
                        _pad_rows(kf.reshape(db, ds, n_heads, hd), page),
                        _pad_rows(vf.reshape(db, ds, n_heads, hd), page),
                        cache_dsa_k, cache_dsa_v, n_pages, sel=sel), db, ds)
                w_out = w_d_out
            x[n] = _mm_res(y, w_out, j, x[n], tm)
            fb = _rmsnorm(x[n], g_ffn[i], BF16, tm)
            hist = None if n == "p" else _expand_history(state_ffn_conv[i])
            tm_up = 2 * ROW_BLOCK if (n == "p" and slen % (2 * ROW_BLOCK) == 0) else tm
            act, st = _ffn_up(fb, w_ffn_up, conv_ffn, i, hist, tm_up, slen)
            outs["fc"][n].append(_last_two(st, nseq[n]))
            x[n] = _mm_res(act, w_ffn_down, i, x[n], tm)

    y_prompt = _rmsnorm(x["p"], g_final, F32, tm_p).reshape(batch, seq, d)
    y_sample = _rmsnorm(x["s"], g_final, F32, ms).reshape(db, ds, d)
    res = [y_prompt, y_sample]
    for name in ("conv", "sbk", "sbv", "dk", "dv", "ak", "av", "ai", "fc"):
        for n in ("p", "s"):
            res.append(jnp.stack(outs[name][n]))
    return tuple(res)
```

```python
import functools
import math

import jax
import jax.numpy as jnp
from jax import lax
from jax.experimental import pallas as pl
from jax.experimental.pallas import tpu as pltpu

NORM_EPS = 1e-6
ROPE_THETA = 10000.0
TOPK_MAX = 256
CONV_W = 3
LANES = 128
SUBLANES = 8
MXU_WIDTH = 256
FLASH_BLOCK = 512
ROW_BLOCK = 512
SB_SKIP_BELOW = -104.0
VMEM_LIMIT_BYTES = 56 * 1024 * 1024
NEG_BIG = -1e30
M_INIT = -1e29
INT_MIN = -2 ** 31
LOG2E = math.log2(math.e)

F32 = jnp.float32
BF16 = jnp.bfloat16


def _params(*sem):
    return pltpu.CompilerParams(dimension_semantics=sem, vmem_limit_bytes=VMEM_LIMIT_BYTES)


def _dot(a, b):
    return jnp.dot(a, b, preferred_element_type=F32)


def _dot_t(a, b):
    return lax.dot_general(a, b, (((1,), (1,)), ((), ())), preferred_element_type=F32)


def _col_chunk(tn):
    return MXU_WIDTH if tn % MXU_WIDTH == 0 else LANES


def _rmsnorm_kernel(x_ref, g_ref, o_ref):
    x = x_ref[...]
    ms = jnp.mean(x * x, axis=-1, keepdims=True)
    o_ref[...] = ((x * lax.rsqrt(ms + NORM_EPS)) * g_ref[...]).astype(o_ref.dtype)


def _rmsnorm(x, g, out_dtype, tm):
    m, d = x.shape
    return pl.pallas_call(
        _rmsnorm_kernel,
        grid=(m // tm,),
        in_specs=[pl.BlockSpec((tm, d), lambda i: (i, 0)), pl.BlockSpec((1, d), lambda i: (0, 0))],
        out_specs=pl.BlockSpec((tm, d), lambda i: (i, 0)),
        out_shape=jax.ShapeDtypeStruct((m, d), out_dtype),
        compiler_params=_params("arbitrary"),
    )(x, g.reshape(1, d))


def _rope_group(y, c, s, half):
    if 2 * half == LANES:
        rot = pltpu.roll(y, half, 1)
    else:
        lane = lax.broadcasted_iota(jnp.int32, y.shape, 1)
        rot = jnp.where(lane % (2 * half) < half, pltpu.roll(y, LANES - half, 1), pltpu.roll(y, half, 1))
    return y * c + rot * s


def _proj_kernel(*refs, half, n_out):
    if half:
        x_ref, w_ref, c_ref, s_ref = refs[:4]
        outs = refs[4:4 + n_out]
    else:
        x_ref, w_ref = refs[:2]
        outs = refs[2:2 + n_out]
    wb_ref = refs[-1]

    @pl.when(pl.program_id(1) == 0)
    def _():
        wb_ref[...] = w_ref[...].astype(BF16)

    x = x_ref[...]
    tn = wb_ref.shape[1]
    ch = _col_chunk(tn)
    for c0 in range(0, tn, ch):
        y = _dot(x, wb_ref[:, c0:c0 + ch])
        if half:
            c = c_ref[...]
            s = s_ref[...]
            for g in range(ch // LANES):
                sl = slice(c0 + g * LANES, c0 + (g + 1) * LANES)
                yg = _rope_group(y[:, g * LANES:(g + 1) * LANES], c, s, half)
                for o in outs:
                    o[:, sl] = yg.astype(o.dtype)
        else:
            for o in outs:
                o[:, c0:c0 + ch] = y.astype(o.dtype)


def _proj(xb, w, layer, col0, ncols, out_dtypes, tm, rope=None):
    m, d = xb.shape
    tn = min(1024, ncols)
    assert ncols % tn == 0 and col0 % tn == 0 and m % tm == 0
    off = col0 // tn
    in_specs = [pl.BlockSpec((tm, d), lambda j, i: (i, 0)),
                pl.BlockSpec((None, d, tn), lambda j, i: (layer, 0, j + off))]
    args = [xb, w]
    half = 0
    if rope is not None:
        ctab, stab, half, tblocks = rope
        in_specs += [pl.BlockSpec((tm, LANES), lambda j, i: (i % tblocks, 0)),
                     pl.BlockSpec((tm, LANES), lambda j, i: (i % tblocks, 0))]
        args += [ctab, stab]
    outs = pl.pallas_call(
        functools.partial(_proj_kernel, half=half, n_out=len(out_dtypes)),
        grid=(ncols // tn, m // tm),
        in_specs=in_specs,
        out_specs=[pl.BlockSpec((tm, tn), lambda j, i: (i, j)) for _ in out_dtypes],
        out_shape=[jax.ShapeDtypeStruct((m, ncols), dt) for dt in out_dtypes],
        scratch_shapes=[pltpu.VMEM((d, tn), BF16)],
        compiler_params=_params("arbitrary", "arbitrary"),
    )(*args)
    return outs


def _mm_res_kernel(a_ref, w_ref, r_ref, o_ref, wb_ref):
    @pl.when(pl.program_id(1) == 0)
    def _():
        wb_ref[...] = w_ref[...].astype(BF16)

    o_ref[...] = r_ref[...] + _dot(a_ref[...], wb_ref[...])


def _mm_res(a, w, layer, res, tm):
    m, k = a.shape
    n = w.shape[2]
    tn = min(512, n)
    return pl.pallas_call(
        _mm_res_kernel,
        grid=(n // tn, m // tm),
        in_specs=[pl.BlockSpec((tm, k), lambda j, i: (i, 0)),
                  pl.BlockSpec((None, k, tn), lambda j, i: (layer, 0, j)),
                  pl.BlockSpec((tm, tn), lambda j, i: (i, j))],
        out_specs=pl.BlockSpec((tm, tn), lambda j, i: (i, j)),
        out_shape=jax.ShapeDtypeStruct((m, n), F32),
        scratch_shapes=[pltpu.VMEM((k, tn), BF16)],
        compiler_params=_params("arbitrary", "arbitrary"),
    )(a, w, res)


def _shifted(u, prev, period):
    row = lax.broadcasted_iota(jnp.int32, u.shape, 0)
    if period == 0:
        p1 = prev[SUBLANES - 1:SUBLANES, :]
        p2 = prev[SUBLANES - 2:SUBLANES - 1, :]
        u1 = jnp.where(row >= 1, pltpu.roll(u, 1, 0), p1)
        u2 = jnp.where(row >= 2, pltpu.roll(u, 2, 0), jnp.where(row == 1, p1, p2))
        return u1, u2
    row = row % period
    u1 = jnp.where(row >= 1, pltpu.roll(u, 1, 0), pltpu.roll(prev, 1, 0))
    u2 = jnp.where(row >= 2, pltpu.roll(u, 2, 0), pltpu.roll(prev, 2, 0))
    return u1, u2


def _dwconv(u, u1, u2, cw):
    return (cw[0:1, :] * u2 + cw[1:2, :] * u1) + cw[2:3, :] * u


def _history(prev_ref, hist_ref, tiles_per_seq):
    if hist_ref is not None:
        return hist_ref[...]

    @pl.when(pl.program_id(1) % tiles_per_seq == 0)
    def _():
        prev_ref[...] = jnp.zeros_like(prev_ref)

    return prev_ref[...]


def _convmix_kernel(*refs, short, tiles_per_seq, period):
    if short:
        x_ref, wb_ref, wc_ref, wh_ref, cw_ref, hist_ref, y_ref, st_ref, wbb, wcb, whb, prev_ref = refs
    else:
        x_ref, wb_ref, wc_ref, wh_ref, cw_ref, y_ref, st_ref, wbb, wcb, whb, prev_ref = refs
        hist_ref = None

    @pl.when(pl.program_id(1) == 0)
    def _():
        wbb[...] = wb_ref[...].astype(BF16)
        wcb[...] = wc_ref[...].astype(BF16)
        whb[...] = wh_ref[...].astype(BF16)

    x = x_ref[...]
    prev_all = _history(prev_ref, hist_ref, tiles_per_seq)
    tn = wbb.shape[1]
    ch = _col_chunk(tn)
    for c0 in range(0, tn, ch):
        sl = slice(c0, c0 + ch)
        u = _dot(x, wcb[:, sl]) * _dot(x, whb[:, sl])
        u1, u2 = _shifted(u, prev_all[:, sl], period)
        conv = _dwconv(u, u1, u2, cw_ref[:, sl])
        y_ref[:, sl] = (_dot(x, wbb[:, sl]) * conv).astype(y_ref.dtype)
        if short:
            st_ref[:, sl] = u
        else:
            prev_ref[:, sl] = u[u.shape[0] - SUBLANES:, :]
            st_ref[:, sl] = u[u.shape[0] - SUBLANES:, :]


def _state_specs(short, tm, tn, tiles_per_seq, col_off=0):
    if short:
        return pl.BlockSpec((tm, tn), lambda j, i: (0, j + col_off))
    return pl.BlockSpec((SUBLANES, tn), lambda j, i: (i // tiles_per_seq, j + col_off))


def _seq_mode(hist_x, m, tm, seq_len):
    if hist_x is not None:
        assert seq_len == SUBLANES and tm == m
        return True, 1, seq_len, m // seq_len, tm
    assert seq_len % tm == 0
    return False, seq_len // tm, 0, m // seq_len, SUBLANES


def _convmix(xb, w_in, conv_w, layer, hist_x, tm, seq_len):
    m, d = xb.shape
    tn = min(512, d)
    nj = d // tn
    short, tiles_per_seq, period, nseq, carry_rows = _seq_mode(hist_x, m, tm, seq_len)
    in_specs = [pl.BlockSpec((tm, d), lambda j, i: (i, 0)),
                pl.BlockSpec((None, d, tn), lambda j, i: (layer, 0, j)),
                pl.BlockSpec((None, d, tn), lambda j, i: (layer, 0, j + nj)),
                pl.BlockSpec((None, d, tn), lambda j, i: (layer, 0, j + 2 * nj)),
                pl.BlockSpec((None, CONV_W, tn), lambda j, i: (layer, 0, j))]
    args = [xb, w_in, w_in, w_in, conv_w]
    if short:
        in_specs.append(pl.BlockSpec((tm, tn), lambda j, i: (0, j)))
        args.append(hist_x)
    y, st = pl.pallas_call(
        functools.partial(_convmix_kernel, short=short, tiles_per_seq=tiles_per_seq, period=period),
        grid=(nj, m // tm),
        in_specs=in_specs,
        out_specs=[pl.BlockSpec((tm, tn), lambda j, i: (i, j)), _state_specs(short, tm, tn, tiles_per_seq)],
        out_shape=[jax.ShapeDtypeStruct((m, d), BF16), jax.ShapeDtypeStruct((nseq * SUBLANES, d), F32)],
        scratch_shapes=[pltpu.VMEM((d, tn), BF16)] * 3 + [pltpu.VMEM((carry_rows, tn), F32)],
        compiler_params=_params("arbitrary", "arbitrary"),
    )(*args)
    return y, st


def _ffn_up_kernel(*refs, short, tiles_per_seq, period):
    if short:
        (x_ref, wg_ref, wu_ref, cg_ref, cu_ref, hg_ref, hu_ref,
         a_ref, sg_ref, su_ref, wgb, wub, pg_ref, pu_ref) = refs
    else:
        x_ref, wg_ref, wu_ref, cg_ref, cu_ref, a_ref, sg_ref, su_ref, wgb, wub, pg_ref, pu_ref = refs
        hg_ref = hu_ref = None

    @pl.when(pl.program_id(1) == 0)
    def _():
        wgb[...] = wg_ref[...].astype(BF16)
        wub[...] = wu_ref[...].astype(BF16)

    prevs = (_history(pg_ref, hg_ref, tiles_per_seq), _history(pu_ref, hu_ref, tiles_per_seq))
    tm = x_ref.shape[0]
    rb = min(tm, ROW_BLOCK)
    tn = wgb.shape[1]
    ch = _col_chunk(tn)
    for c0 in range(0, tn, ch):
        sl = slice(c0, c0 + ch)
        prev = [prevs[0][:, sl], prevs[1][:, sl]]
        for r0 in range(0, tm, rb):
            x = x_ref[r0:r0 + rb, :]
            halves = []
            for h, (w_b, c_ref, s_ref) in enumerate(((wgb, cg_ref, sg_ref), (wub, cu_ref, su_ref))):
                up = _dot(x, w_b[:, sl])
                u1, u2 = _shifted(up, prev[h], period)
                halves.append(_dwconv(up, u1, u2, c_ref[:, sl]))
                if short:
                    s_ref[:, sl] = up
                else:
                    prev[h] = up[rb - SUBLANES:, :]
            g, u = halves
            a_ref[r0:r0 + rb, sl] = ((g * jax.nn.sigmoid(g)) * u).astype(a_ref.dtype)
        if not short:
            for h, (p_ref, s_ref) in enumerate(((pg_ref, sg_ref), (pu_ref, su_ref))):
                p_ref[:, sl] = prev[h]
                s_ref[:, sl] = prev[h]


def _ffn_up(xb, w_up, conv_w, layer, hist_x, tm, seq_len):
    m, d = xb.shape
    d_ff = w_up.shape[2] // 2
    tn = 512 if d_ff % 512 == 0 else LANES
    nj = d_ff // tn
    short, tiles_per_seq, period, nseq, carry_rows = _seq_mode(hist_x, m, tm, seq_len)
    in_specs = [pl.BlockSpec((tm, d), lambda j, i: (i, 0)),
                pl.BlockSpec((None, d, tn), lambda j, i: (layer, 0, j)),
                pl.BlockSpec((None, d, tn), lambda j, i: (layer, 0, j + nj)),
                pl.BlockSpec((None, CONV_W, tn), lambda j, i: (layer, 0, j)),
                pl.BlockSpec((None, CONV_W, tn), lambda j, i: (layer, 0, j + nj))]
    args = [xb, w_up, w_up, conv_w, conv_w]
    if short:
        in_specs += [pl.BlockSpec((tm, tn), lambda j, i: (0, j)), pl.BlockSpec((tm, tn), lambda j, i: (0, j + nj))]
        args += [hist_x, hist_x]
    act, sg, su = pl.pallas_call(
        functools.partial(_ffn_up_kernel, short=short, tiles_per_seq=tiles_per_seq, period=period),
        grid=(nj, m // tm),
        in_specs=in_specs,
        out_specs=[pl.BlockSpec((tm, tn), lambda j, i: (i, j)),
                   _state_specs(short, tm, tn, tiles_per_seq),
                   _state_specs(short, tm, tn, tiles_per_seq)],
        out_shape=[jax.ShapeDtypeStruct((m, d_ff), BF16),
                   jax.ShapeDtypeStruct((nseq * SUBLANES, d_ff), F32),
                   jax.ShapeDtypeStruct((nseq * SUBLANES, d_ff), F32)],
        scratch_shapes=[pltpu.VMEM((d, tn), BF16)] * 2 + [pltpu.VMEM((carry_rows, tn), F32)] * 2,
        compiler_params=_params("arbitrary", "arbitrary"),
    )(*args)
    return act, jnp.concatenate([sg, su], axis=1)


def _log_sigmoid_pair(z):
    l1p = jnp.log(1.0 + jnp.exp(-jnp.abs(z)))
    ls = jnp.minimum(z, 0.0) - l1p
    return ls, ls - z


def _suffix_sum(x, tri):
    hi = x.astype(BF16)
    lo = (x - hi.astype(F32)).astype(BF16)
    return _dot(hi, tri) + _dot(lo, tri)


def _tri(n):
    j = lax.broadcasted_iota(jnp.int32, (n, n), 0)
    s = lax.broadcasted_iota(jnp.int32, (n, n), 1)
    return (j > s).astype(BF16)


def _sb_prompt_kernel(q_ref, k_ref, v_ref, tri_ref, o_ref, *, tq, hd, scale):
    qi = pl.program_id(2)
    tri = tri_ref[...]
    n_h = q_ref.shape[1] // hd

    def block(ki, h, carry, acc, diag):
        start = pl.multiple_of(ki * tq, tq)
        hs = slice(h * hd, (h + 1) * hd)
        k = k_ref[pl.ds(start, tq), hs]
        v = v_ref[pl.ds(start, tq), hs]
        z = _dot_t(q_ref[:, hs], k) * scale
        ls, l1m = _log_sigmoid_pair(z)
        if diag:
            row = lax.broadcasted_iota(jnp.int32, z.shape, 0)
            col = lax.broadcasted_iota(jnp.int32, z.shape, 1)
            valid = col < row
            l1m = jnp.where(valid, l1m, 0.0)
        after = _suffix_sum(l1m, tri)
        a = jnp.exp(ls + after + carry)
        if diag:
            a = jnp.where(valid, a, 0.0)
        acc = acc + _dot(a.astype(BF16), v)
        carry = carry + jnp.sum(l1m, axis=1, keepdims=True)
        return carry, acc

    state = tuple(block(qi, h, jnp.zeros((tq, 1), F32), jnp.zeros((tq, hd), F32), True) for h in range(n_h))

    def more(c):
        top = c[1][0][0]
        for h in range(1, n_h):
            top = jnp.maximum(top, c[1][h][0])
        return (c[0] < qi) & (jnp.max(top) > SB_SKIP_BELOW)

    def body(c):
        return c[0] + 1, tuple(block(qi - 1 - c[0], h, c[1][h][0], c[1][h][1], False) for h in range(n_h))

    _, state = lax.while_loop(more, body, (jnp.int32(0), state))
    for h in range(n_h):
        o_ref[:, h * hd:(h + 1) * hd] = state[h][1].astype(o_ref.dtype)


def _sb_prompt(qb, kb, vb, batch, seq, n_heads, hd):
    m, d = qb.shape
    tq = min(256, seq)
    nq = seq // tq
    n_h = 2 if n_heads % 2 == 0 else 1
    w = n_h * hd
    return pl.pallas_call(
        functools.partial(_sb_prompt_kernel, tq=tq, hd=hd, scale=hd ** -0.5),
        grid=(batch, n_heads // n_h, nq),
        in_specs=[pl.BlockSpec((tq, w), lambda b, h, q: (b * nq + q, h)),
                  pl.BlockSpec((seq, w), lambda b, h, q: (b, h)),
                  pl.BlockSpec((seq, w), lambda b, h, q: (b, h)),
                  pl.BlockSpec((tq, tq), lambda b, h, q: (0, 0))],
        out_specs=pl.BlockSpec((tq, w), lambda b, h, q: (b * nq + q, h)),
        out_shape=jax.ShapeDtypeStruct((m, d), BF16),
        compiler_params=_params("arbitrary", "arbitrary", "arbitrary"),
    )(qb, kb, vb, _tri(tq))


def _online_softmax_step(s, mask, m, l, acc, v):
    if mask is not None:
        s = jnp.where(mask, s, NEG_BIG)
    m_new = jnp.maximum(m, jnp.max(s, axis=1, keepdims=True))
    p = jnp.exp2(s - m_new)
    if mask is not None:
        p = jnp.where(mask, p, 0.0)
    alpha = jnp.exp2(m - m_new)
    l = alpha * l + jnp.sum(p, axis=1, keepdims=True)
    acc = alpha * acc + _dot(p.astype(BF16), v)
    return m_new, l, acc


def _lambda_full(lam_ref, lambda_init):
    lf = lam_ref[...]
    s01 = jnp.sum(lf[0:1, :] * lf[1:2, :], axis=1, keepdims=True)
    s23 = jnp.sum(lf[2:3, :] * lf[3:4, :], axis=1, keepdims=True)
    return jnp.exp(s01) - jnp.exp(s23) + lambda_init


def _sub_norm(o, g, lambda_init):
    ms = jnp.mean(o * o, axis=-1, keepdims=True)
    return ((o * lax.rsqrt(ms + NORM_EPS)) * g) * (1.0 - lambda_init)


def _diff_prompt_kernel(q_ref, k_ref, v_ref, lam_ref, g_ref, o_ref, *, tq, hd, scale, lambda_init):
    qi = pl.program_id(2)
    dv = v_ref.shape[1]

    def block(ki, state, diag):
        start = pl.multiple_of(ki * tq, tq)
        v = v_ref[pl.ds(start, tq), :]
        mask = None
        if diag:
            row = lax.broadcasted_iota(jnp.int32, (tq, tq), 0)
            col = lax.broadcasted_iota(jnp.int32, (tq, tq), 1)
            mask = col <= row
        new = []
        for i in range(2):
            q = q_ref[:, i * hd:(i + 1) * hd]
            k = k_ref[pl.ds(start, tq), i * hd:(i + 1) * hd]
            s = _dot_t(q, k) * (scale * LOG2E)
            new.append(_online_softmax_step(s, mask, *state[i], v))
        return tuple(new)

    init = tuple((jnp.full((tq, 1), NEG_BIG, F32), jnp.zeros((tq, 1), F32), jnp.zeros((tq, dv), F32))
                 for _ in range(2))
    state = lax.fori_loop(0, qi, lambda ki, st: block(ki, st, False), init)
    state = block(qi, state, True)
    lam = _lambda_full(lam_ref, lambda_init)
    (_, l0, a0), (_, l1, a1) = state
    o = a0 / l0 - lam * (a1 / l1)
    o_ref[...] = _sub_norm(o, g_ref[...], lambda_init).astype(o_ref.dtype)


def _diff_prompt(qb, kb, vb, lam, g_sub, batch, seq, n_dh, hd, lambda_init):
    m, d = qb.shape
    tq = min(FLASH_BLOCK, seq)
    nq = seq // tq
    w = 2 * hd
    return pl.pallas_call(
        functools.partial(_diff_prompt_kernel, tq=tq, hd=hd, scale=hd ** -0.5, lambda_init=lambda_init),
        grid=(batch, n_dh, nq),
        in_specs=[pl.BlockSpec((tq, w), lambda b, h, q: (b * nq + q, h)),
                  pl.BlockSpec((seq, w), lambda b, h, q: (b, h)),
                  pl.BlockSpec((seq, w), lambda b, h, q: (b, h)),
                  pl.BlockSpec(lam.shape, lambda b, h, q: (0, 0)),
                  pl.BlockSpec((1, w), lambda b, h, q: (0, 0))],
        out_specs=pl.BlockSpec((tq, w), lambda b, h, q: (b * nq + q, h)),
        out_shape=jax.ShapeDtypeStruct((m, d), BF16),
        compiler_params=_params("arbitrary", "arbitrary", "arbitrary"),
    )(qb, kb, vb, lam, g_sub.reshape(1, w))


def _dsa_prompt_kernel(q_ref, k_ref, v_ref, sel_ref, o_ref, *, tq, hd, scale):
    qi = pl.program_id(2)
    n_h = q_ref.shape[1] // hd

    def block(ki, state):
        start = pl.multiple_of(ki * tq, tq)
        bias = sel_ref[:, pl.ds(start, tq)].astype(F32)
        new = []
        for h in range(n_h):
            hs = slice(h * hd, (h + 1) * hd)
            s = _dot_t(q_ref[:, hs], k_ref[pl.ds(start, tq), hs]) * (scale * LOG2E) + bias
            new.append(_online_softmax_step(s, None, *state[h], v_ref[pl.ds(start, tq), hs]))
        return tuple(new)

    init = tuple((jnp.full((tq, 1), M_INIT, F32), jnp.zeros((tq, 1), F32), jnp.zeros((tq, hd), F32))
                 for _ in range(n_h))
    state = lax.fori_loop(0, qi + 1, block, init)
    for h in range(n_h):
        _, l, acc = state[h]
        o_ref[:, h * hd:(h + 1) * hd] = (acc / l).astype(o_ref.dtype)


def _dsa_prompt(qb, kb, vb, sel, batch, seq, n_heads, hd):
    m, d = qb.shape
    tq = min(FLASH_BLOCK, seq)
    nq = seq // tq
    n_h = 2 if n_heads % 2 == 0 else 1
    w = n_h * hd
    return pl.pallas_call(
        functools.partial(_dsa_prompt_kernel, tq=tq, hd=hd, scale=hd ** -0.5),
        grid=(batch, n_heads // n_h, nq),
        in_specs=[pl.BlockSpec((tq, w), lambda b, h, q: (b * nq + q, h)),
                  pl.BlockSpec((seq, w), lambda b, h, q: (b, h)),
                  pl.BlockSpec((seq, w), lambda b, h, q: (b, h)),
                  pl.BlockSpec((tq, seq), lambda b, h, q: (b * nq + q, 0))],
        out_specs=pl.BlockSpec((tq, w), lambda b, h, q: (b * nq + q, h)),
        out_shape=jax.ShapeDtypeStruct((m, d), BF16),
        compiler_params=_params("arbitrary", "arbitrary", "arbitrary"),
    )(qb, kb, vb, sel)


def _order_key(x):
    bits = pltpu.bitcast(x + 0.0, jnp.int32)
    return jnp.where(bits >= 0, bits, bits ^ jnp.int32(0x7FFFFFFF))


def _kth_largest_key(count_ge, rows, k):
    zero = jnp.zeros((rows, 1), jnp.int32)
    t0 = jnp.where(count_ge(zero) >= k, zero, jnp.int32(INT_MIN))

    def body(it, t):
        cand = t | jnp.left_shift(jnp.int32(1), jnp.int32(30) - it)
        return jnp.where(count_ge(cand) >= k, cand, t)

    return lax.fori_loop(0, 31, body, t0)


def _store_topk_mask(o_ref, key, valid, k, t):
    ge = valid & (key >= t)
    n_ge = jnp.sum(ge.astype(jnp.int32), axis=1, keepdims=True)
    has_ties = jnp.max(n_ge) > k

    @pl.when(jnp.logical_not(has_ties))
    def _():
        o_ref[...] = jnp.where(ge, 0.0, NEG_BIG).astype(o_ref.dtype)

    @pl.when(has_ties)
    def _():
        gt = key > t
        eq = key == t
        need = k - jnp.sum(gt.astype(jnp.int32), axis=1, keepdims=True)
        col = lax.broadcasted_iota(jnp.int32, key.shape, 1)
        nbits = int(key.shape[1]).bit_length()

        def body(it, j):
            cand = j | jnp.left_shift(jnp.int32(1), jnp.int32(nbits - 1) - it)
            c = jnp.sum((eq & (col < cand)).astype(jnp.int32), axis=1, keepdims=True)
            return jnp.where(c < need, cand, j)

        j = lax.fori_loop(0, nbits, body, jnp.zeros_like(t))
        sel = valid & (gt | (eq & (col <= j)))
        o_ref[...] = jnp.where(sel, 0.0, NEG_BIG).astype(o_ref.dtype)


def _select_prompt_kernel(iq_ref, ik_ref, iw_ref, o_ref, key_ref, qh_ref, *, tq, cw, n_ih, idim, k_top, wscale):
    qi = pl.program_id(1)
    seq = key_ref.shape[1]
    n_need = (qi * tq) // cw + 1
    iw = iw_ref[...] * wscale
    lane = lax.broadcasted_iota(jnp.int32, (tq, LANES), 1)
    per_group = LANES // idim
    for h in range(n_ih):
        g, r = divmod(h, per_group)
        qg = iq_ref[:, g * LANES:(g + 1) * LANES].astype(F32)
        qh_ref[h] = jnp.where((lane >= r * idim) & (lane < (r + 1) * idim), qg, 0.0).astype(BF16)

    def chunk_at(c):
        return pl.ds(pl.multiple_of(c * cw, cw), cw)

    def score_chunk(c, carry):
        ikc = ik_ref[chunk_at(c), :]
        acc = None
        for h in range(n_ih):
            s = jnp.maximum(_dot_t(qh_ref[h], ikc), 0.0) * iw[:, idim + h:idim + h + 1]
            acc = s if acc is None else acc + s
        row = qi * tq + lax.broadcasted_iota(jnp.int32, (tq, cw), 0)
        col = c * cw + lax.broadcasted_iota(jnp.int32, (tq, cw), 1)
        key_ref[:, chunk_at(c)] = _order_key(jnp.where(col <= row, acc, -jnp.inf))
        return carry

    def fill_chunk(c, carry):
        key_ref[:, chunk_at(c)] = jnp.full((tq, cw), INT_MIN, jnp.int32)
        return carry

    lax.fori_loop(0, n_need, score_chunk, 0)
    lax.fori_loop(n_need, seq // cw, fill_chunk, 0)

    def count_ge(t):
        def body(c, acc):
            m = (key_ref[:, chunk_at(c)] >= t).astype(jnp.int32)
            for v in range(cw // LANES):
                acc = acc + m[:, v * LANES:(v + 1) * LANES]
            return acc

        acc = lax.fori_loop(0, n_need, body, jnp.zeros((tq, LANES), jnp.int32))
        return jnp.sum(acc, axis=1, keepdims=True)

    t = _kth_largest_key(count_ge, tq, k_top)
    row = qi * tq + lax.broadcasted_iota(jnp.int32, (tq, seq), 0)
    col = lax.broadcasted_iota(jnp.int32, (tq, seq), 1)
    _store_topk_mask(o_ref, key_ref[...], col <= row, k_top, t)


def _select_prompt(iqb, ik2b, tail, batch, seq, n_ih, idim, k_top):
    m = iqb.shape[0]
    tq = min(128, seq)
    cw = min(512, seq)
    nq = seq // tq
    assert LANES % idim == 0 and cw % tq == 0 and seq % cw == 0
    return pl.pallas_call(
        functools.partial(_select_prompt_kernel, tq=tq, cw=cw, n_ih=n_ih, idim=idim, k_top=k_top,
                          wscale=(idim ** -0.5) * (n_ih ** -0.5)),
        grid=(batch, nq),
        in_specs=[pl.BlockSpec((tq, n_ih * idim), lambda b, q: (b * nq + q, 0)),
                  pl.BlockSpec((seq, LANES), lambda b, q: (b, 0)),
                  pl.BlockSpec((tq, LANES), lambda b, q: (b * nq + q, 0))],
        out_specs=pl.BlockSpec((tq, seq), lambda b, q: (b * nq + q, 0)),
        out_shape=jax.ShapeDtypeStruct((m, seq), BF16),
        scratch_shapes=[pltpu.VMEM((tq, seq), jnp.int32), pltpu.VMEM((n_ih, tq, LANES), BF16)],
        compiler_params=_params("arbitrary", "arbitrary"),
    )(iqb, ik2b, tail)


def _sel_sample_kernel(pt_ref, qm_ref, w_ref, iknew_ref, *refs, pages_per_step, n_pages, ds, k_top):
    ik_refs = refs[:pages_per_step]
    o_ref, sc_ref = refs[pages_per_step:pages_per_step + 2]
    s_id = pl.program_id(1)
    qm = qm_ref[...]
    w = w_ref[...]
    page = ik_refs[0].shape[0]

    def scores(ik):
        s = jnp.maximum(_dot_t(qm, ik.astype(BF16)), 0.0) * w
        tot = s[0:ds, :]
        for h in range(1, s.shape[0] // ds):
            tot = tot + s[h * ds:(h + 1) * ds, :]
        return tot

    for c in range(pages_per_step):
        pg = s_id * pages_per_step + c
        sc_ref[:, pl.ds(pl.multiple_of(pg * page, page), page)] = scores(ik_refs[c][...])

    @pl.when(s_id == pl.num_programs(1) - 1)
    def _():
        t = lax.broadcasted_iota(jnp.int32, (ds, page), 0)
        j = lax.broadcasted_iota(jnp.int32, (ds, page), 1)
        sc_ref[:, n_pages * page:] = jnp.where(j <= t, scores(iknew_ref[...]), -jnp.inf)
        col = lax.broadcasted_iota(jnp.int32, sc_ref.shape, 1)
        row = lax.broadcasted_iota(jnp.int32, sc_ref.shape, 0)
        key = _order_key(sc_ref[...])

        def count_ge(thr):
            return jnp.sum((key >= thr).astype(jnp.int32), axis=1, keepdims=True)

        thr = _kth_largest_key(count_ge, ds, k_top)
        _store_topk_mask(o_ref, key, col <= n_pages * page + row, k_top, thr)


def _select_sample(page_table, qm, wcol, iknew, pool, layer, n_pages, k_top):
    db, rows, idim = qm.shape
    page = pool.shape[2]
    ds = SUBLANES
    pps = 8 if n_pages % 8 == 0 else (4 if n_pages % 4 == 0 else 1)
    n_steps = n_pages // pps
    nk = (n_pages + 1) * page

    def pool_spec(c):
        return pl.BlockSpec((None, None, page, idim), lambda b, s, pt: (layer, pt[b, s * pps + c], 0, 0))

    grid_spec = pltpu.PrefetchScalarGridSpec(
        num_scalar_prefetch=1,
        grid=(db, n_steps),
        in_specs=[pl.BlockSpec((None, rows, idim), lambda b, s, pt: (b, 0, 0)),
                  pl.BlockSpec((None, rows, 1), lambda b, s, pt: (b, 0, 0)),
                  pl.BlockSpec((None, page, idim), lambda b, s, pt: (b, 0, 0))]
                 + [pool_spec(c) for c in range(pps)],
        out_specs=pl.BlockSpec((None, ds, nk), lambda b, s, pt: (b, 0, 0)),
        scratch_shapes=[pltpu.VMEM((ds, nk), F32)],
    )
    return pl.pallas_call(
        functools.partial(_sel_sample_kernel, pages_per_step=pps, n_pages=n_pages, ds=ds, k_top=k_top),
        grid_spec=grid_spec,
        out_shape=jax.ShapeDtypeStruct((db, ds, nk), BF16),
        compiler_params=_params("arbitrary", "arbitrary"),
    )(page_table, qm, wcol, iknew, *([pool] * pps))


def _class_reduce(x, n_cls, op):
    return _class_finish(_lane_fold(x, op), n_cls, op)


def _lane_fold(x, op):
    r = x[:, 0:LANES]
    for v in range(1, x.shape[1] // LANES):
        r = op(r, x[:, v * LANES:(v + 1) * LANES])
    return r


def _class_finish(r, n_cls, op):
    sh = LANES // 2
    while sh >= n_cls:
        r = op(r, pltpu.roll(r, sh, 1))
        sh //= 2
    return r


def _tile_lanes(r, ncol):
    return jnp.concatenate([r] * (ncol // LANES), axis=1)


def _class_column(r, n_cls):
    return jnp.concatenate([r[:, c:c + 1] for c in range(n_cls)], axis=0)


def _suffix_by_class(x, n_cls):
    lane = lax.broadcasted_iota(jnp.int32, (x.shape[0], LANES), 1)
    nv = x.shape[1] // LANES
    excl_in, tots = [], []
    for v in range(nv):
        xv = x[:, v * LANES:(v + 1) * LANES]
        inc = xv
        sh = n_cls
        while sh < LANES:
            inc = inc + jnp.where(lane < LANES - sh, pltpu.roll(inc, LANES - sh, 1), 0.0)
            sh *= 2
        tot = xv
        sh = LANES // 2
        while sh >= n_cls:
            tot = tot + pltpu.roll(tot, sh, 1)
            sh //= 2
        excl_in.append(inc - xv)
        tots.append(tot)
    later = jnp.zeros((x.shape[0], LANES), F32)
    out = [None] * nv
    for v in reversed(range(nv)):
        out[v] = excl_in[v] + later
        later = later + tots[v]
    return jnp.concatenate(out, axis=1), later


def _paged_kernel(pt_ref, *refs, kind, pages_per_step, n_cls, n_grp, scale, lambda_init):
    n_in = 3 + (2 if kind == "dsa" else 0) + 2 * pages_per_step + (2 if kind == "diff" else 0)
    ins, o_ref, (m_ref, l_ref, acc_ref) = refs[:n_in], refs[n_in], refs[n_in + 1:]
    q_ref, knew_ref, vnew_ref = ins[:3]
    pos = 3
    if kind == "dsa":
        selnew_ref, sel_ref = ins[pos:pos + 2]
        pos += 2
    k_refs = ins[pos:pos + pages_per_step]
    v_refs = ins[pos + pages_per_step:pos + 2 * pages_per_step]
    pos += 2 * pages_per_step
    if kind == "diff":
        lam_ref, g_ref = ins[pos:pos + 2]
    s_id = pl.program_id(1)
    page = knew_ref.shape[0]
    ncol = page * n_cls
    col = lax.broadcasted_iota(jnp.int32, (SUBLANES, ncol), 1)
    cls = col % n_cls

    def fold(s_big):
        out = s_big[0:SUBLANES, :]
        for c in range(1, n_cls):
            out = jnp.where(cls == c, s_big[c * SUBLANES:(c + 1) * SUBLANES, :], out)
        return out

    def unfold(p):
        return jnp.concatenate([jnp.where(cls == c, p, 0.0) for c in range(n_cls)], axis=0).astype(BF16)

    def widen(x):
        lane = lax.broadcasted_iota(jnp.int32, (SUBLANES, LANES), 1)
        per = LANES // n_cls
        return jnp.concatenate([jnp.take_along_axis(x, v * per + lane // n_cls, axis=1)
                                for v in range(ncol // LANES)], axis=1)

    def process(pages):
        v2s = [v_ref[...].reshape(ncol, v_ref.shape[-1]).astype(BF16) for _, v_ref, _, _ in pages]
        weights, alphas = [], []
        for g in range(n_grp):
            ss = []
            for k_ref, _, mask, bias in pages:
                k = k_ref[...] if n_grp == 1 else k_ref[:, pl.ds(g, n_cls, stride=n_grp), :]
                s = fold(_dot_t(q_ref[g], k.reshape(ncol, k.shape[-1]).astype(BF16)) * scale)
                if bias is not None:
                    s = s + bias
                if mask is not None and kind != "sb":
                    s = jnp.where(mask, s, NEG_BIG)
                ss.append(s)
            ws = []
            if kind == "sb":
                carry = m_ref[g]
                for (_, _, mask, _), s in zip(pages, ss):
                    ls, l1m = _log_sigmoid_pair(s)
                    if mask is not None:
                        l1m = jnp.where(mask, l1m, 0.0)
                    excl, total = _suffix_by_class(l1m, n_cls)
                    a = jnp.exp(ls + excl + _tile_lanes(carry, ncol))
                    ws.append(a if mask is None else jnp.where(mask, a, 0.0))
                    carry = carry + total
                m_ref[g] = carry
            else:
                m_old = m_ref[g]
                m_blk = None
                for s in ss:
                    r = _lane_fold(s, jnp.maximum)
                    m_blk = r if m_blk is None else jnp.maximum(m_blk, r)
                m_new = jnp.maximum(m_old, _class_finish(m_blk, n_cls, jnp.maximum))
                mt = _tile_lanes(m_new, ncol)
                l_blk = None
                for (_, _, mask, _), s in zip(pages, ss):
                    p = jnp.exp(s - mt)
                    if mask is not None:
                        p = jnp.where(mask, p, 0.0)
                    r = _lane_fold(p, jnp.add)
                    l_blk = r if l_blk is None else l_blk + r
                    ws.append(p)
                alpha = jnp.exp(m_old - m_new)
                l_ref[g] = alpha * l_ref[g] + _class_finish(l_blk, n_cls, jnp.add)
                m_ref[g] = m_new
                alphas.append(_class_column(alpha, n_cls))
            weights.append(ws)
        acc_blk = None
        for c, v2 in enumerate(v2s):
            big = [unfold(weights[g][c]) for g in range(n_grp)]
            part = _dot(big[0] if n_grp == 1 else jnp.concatenate(big, axis=0), v2)
            acc_blk = part if acc_blk is None else acc_blk + part
        if kind == "sb":
            acc_ref[...] += acc_blk
        else:
            alpha_col = alphas[0] if n_grp == 1 else jnp.concatenate(alphas, axis=0)
            acc_ref[...] = acc_ref[...] * alpha_col + acc_blk

    @pl.when(s_id == 0)
    def _():
        acc_ref[...] = jnp.zeros_like(acc_ref)
        l_ref[...] = jnp.zeros_like(l_ref)
        m_ref[...] = jnp.full_like(m_ref, {"sb": 0.0, "diff": NEG_BIG, "dsa": M_INIT}[kind])
        t = lax.broadcasted_iota(jnp.int32, (SUBLANES, ncol), 0)
        slot = col // n_cls
        if kind == "sb":
            process([(knew_ref, vnew_ref, slot < t, None)])
        elif kind == "diff":
            process([(knew_ref, vnew_ref, slot <= t, None)])
        else:
            process([(knew_ref, vnew_ref, None, widen(selnew_ref[...].astype(F32)))])

    pages = []
    for c in range(pages_per_step):
        bias = None
        if kind == "dsa":
            lo = (pages_per_step - 1 - c) * page
            bias = widen(sel_ref[:, lo:lo + page].astype(F32))
        pages.append((k_refs[c], v_refs[c], None, bias))
    process(pages)

    @pl.when(s_id == pl.num_programs(1) - 1)
    def _():
        if kind == "sb":
            o_ref[...] = acc_ref[...].astype(o_ref.dtype)
        elif kind == "dsa":
            o_ref[...] = (acc_ref[...] / _class_column(l_ref[0], n_cls)).astype(o_ref.dtype)
        else:
            lam = _lambda_full(lam_ref, lambda_init)
            rows = n_cls * SUBLANES
            a0 = acc_ref[0:rows, :] / _class_column(l_ref[0], n_cls)
            a1 = acc_ref[rows:2 * rows, :] / _class_column(l_ref[1], n_cls)
            o_ref[...] = _sub_norm(a0 - lam * a1, g_ref[...], lambda_init).astype(o_ref.dtype)


def _paged_attention(kind, page_table, layer, q, knew, vnew, kpool, vpool, n_pages, *,
                     sel=None, lam=None, g_sub=None, lambda_init=0.0):
    db, n_grp, rows, hd = q.shape
    n_cls = rows // SUBLANES
    page = kpool.shape[2]
    dv = vpool.shape[4]
    ncol = page * n_cls
    pps = 8 if n_pages % 8 == 0 else (4 if n_pages % 4 == 0 else 1)
    n_steps = n_pages // pps

    def pool_spec(pool, c):
        return pl.BlockSpec((None, None) + pool.shape[2:],
                            lambda b, s, pt: (layer, pt[b, n_pages - 1 - (s * pps + c)], 0, 0, 0))

    in_specs = [pl.BlockSpec((None,) + q.shape[1:], lambda b, s, pt: (b, 0, 0, 0)),
                pl.BlockSpec((None,) + knew.shape[1:], lambda b, s, pt: (b, 0, 0, 0)),
                pl.BlockSpec((None,) + vnew.shape[1:], lambda b, s, pt: (b, 0, 0, 0))]
    args = [q, knew, vnew]
    if kind == "dsa":
        in_specs += [pl.BlockSpec((None, SUBLANES, page), lambda b, s, pt: (b, 0, n_pages)),
                     pl.BlockSpec((None, SUBLANES, pps * page), lambda b, s, pt: (b, 0, n_steps - 1 - s))]
        args += [sel, sel]
    in_specs += [pool_spec(kpool, c) for c in range(pps)] + [pool_spec(vpool, c) for c in range(pps)]
    args += [kpool] * pps + [vpool] * pps
    if kind == "diff":
        in_specs += [pl.BlockSpec(lam.shape, lambda b, s, pt: (0, 0)),
                     pl.BlockSpec((1, dv), lambda b, s, pt: (0, 0))]
        args += [lam, g_sub.reshape(1, dv)]
    grid_spec = pltpu.PrefetchScalarGridSpec(
        num_scalar_prefetch=1,
        grid=(db, n_steps),
        in_specs=in_specs,
        out_specs=pl.BlockSpec((None, rows, dv), lambda b, s, pt: (b, 0, 0)),
        scratch_shapes=[pltpu.VMEM((n_grp, SUBLANES, LANES), F32), pltpu.VMEM((n_grp, SUBLANES, LANES), F32),
                        pltpu.VMEM((n_grp * rows, dv), F32)],
    )
    return pl.pallas_call(
        functools.partial(_paged_kernel, kind=kind, pages_per_step=pps, n_cls=n_cls, n_grp=n_grp,
                          scale=hd ** -0.5, lambda_init=lambda_init),
        grid_spec=grid_spec,
        out_shape=jax.ShapeDtypeStruct((db, rows, dv), BF16),
        compiler_params=_params("arbitrary", "arbitrary"),
    )(page_table, *args)


def _rope_tables(pos, half, ident_from=None):
    inv = ROPE_THETA ** (-jnp.arange(half, dtype=F32) / half)
    ang = pos.astype(F32)[:, None] * inv
    cos, sin = jnp.cos(ang), jnp.sin(ang)
    reps = LANES // (2 * half)
    c = jnp.tile(jnp.concatenate([cos, cos], axis=1), (1, reps))
    s = jnp.tile(jnp.concatenate([-sin, sin], axis=1), (1, reps))
    if ident_from is not None:
        lane = jnp.arange(LANES)[None, :]
        c = jnp.where(lane < ident_from, c, 1.0)
        s = jnp.where(lane < ident_from, s, 0.0)
    return c, s


def _expand_history(hist):
    nseq, _, c = hist.shape
    z = jnp.zeros((nseq, SUBLANES - 2, c), hist.dtype)
    blocks = jnp.concatenate([z, hist], axis=1).reshape(nseq * SUBLANES, c)
    return jnp.roll(blocks, -SUBLANES, axis=0)


def _last_two(st, nseq):
    return st.reshape(nseq, SUBLANES, st.shape[1])[:, SUBLANES - 2:, :]


def _pad_rows(x, rows):
    return jnp.pad(x, ((0, 0), (0, rows - x.shape[1])) + ((0, 0),) * (x.ndim - 2))


def _rows_by_class(x, db, ds, n_grp, n_cls):
    hd = x.shape[1] // (n_cls * n_grp)
    return x.reshape(db, ds, n_cls, n_grp, hd).transpose(0, 3, 2, 1, 4).reshape(db, n_grp, n_cls * ds, hd)


def _rows_by_query(y, db, ds):
    n_cls, dv = y.shape[1] // ds, y.shape[2]
    return y.reshape(db, n_cls, ds, dv).transpose(0, 2, 1, 3).reshape(db * ds, n_cls * dv)


def kernel(x_prompt, x_sample, state_conv_mix, cache_sb_k, cache_sb_v, cache_diff_k, cache_diff_v, cache_dsa_k, cache_dsa_v, cache_dsa_idx_k, state_ffn_conv, page_table, g_mix, g_ffn, g_final, w_a_in, conv_a, w_a_out, w_b_qkv, w_b_out, w_c_qkv, lam_c, g_c_subln, w_c_out, w_d_in, w_d_out, w_ffn_up, conv_ffn, w_ffn_down):
    batch, seq, d = x_prompt.shape
    db, ds, _ = x_sample.shape
    depth = g_mix.shape[0]
    n_pool, page = cache_sb_k.shape[1], cache_sb_k.shape[2]
    n_heads, hd = cache_sb_k.shape[3], cache_sb_k.shape[4]
    n_dh = cache_diff_k.shape[3]
    idim = cache_dsa_idx_k.shape[3]
    n_pages = page_table.shape[1]
    past = n_pages * page
    n_ih = (w_d_in.shape[2] - 3 * d - idim) // (idim + 1)
    assert ds == SUBLANES and hd == LANES and cache_diff_k.shape[5] == hd and n_dh * 2 == n_heads
    assert (n_ih * idim) % LANES == 0 and idim + n_ih <= LANES

    mp, ms = batch * seq, db * ds
    tm_p = min(512, seq)
    streams = (("p", tm_p, seq), ("s", ms, ds))
    x = {"p": x_prompt.reshape(mp, d), "s": x_sample.reshape(ms, d)}
    pos = {"p": jnp.arange(seq), "s": past + jnp.arange(ds)}
    pos_rows = {"p": pos["p"], "s": jnp.tile(pos["s"], db)}
    rope_blocks = {"p": seq // tm_p, "s": 1}
    rope_hd = {n: _rope_tables(pos_rows[n], hd // 2) for n in x}
    rope_idx = {n: _rope_tables(pos_rows[n], idim // 2) for n in x}
    rope_tail = {n: _rope_tables(pos_rows[n], idim // 2, ident_from=idim) for n in x}
    k_top = {"p": min(TOPK_MAX, seq // 4), "s": min(TOPK_MAX, (past + ds) // 4)}
    nseq = {"p": batch, "s": db}

    outs = {name: {"p": [], "s": []} for name in
            ("conv", "sbk", "sbv", "dk", "dv", "ak", "av", "ai", "fc")}

    def shaped(a, n, tail):
        return a.reshape((nseq[n], seq if n == "p" else ds) + tail)

    for i in range(depth):
        kind, j = i % 4, i // 4
        for n, tm, slen in streams:
            xb = _rmsnorm(x[n], g_mix[i], BF16, tm)
            if kind == 0:
                hist = None if n == "p" else _expand_history(state_conv_mix[j])
                y, st = _convmix(xb, w_a_in, conv_a, j, hist, tm, slen)
                outs["conv"][n].append(_last_two(st, nseq[n]))
                w_out = w_a_out
            elif kind == 1:
                w = w_b_qkv
                (qb,) = _proj(xb, w, j, 0, d, (BF16,), tm)
                kf, kb = _proj(xb, w, j, d, d, (F32, BF16), tm)
                vf, vb = _proj(xb, w, j, 2 * d, d, (F32, BF16), tm)
                outs["sbk"][n].append(shaped(kf, n, (n_heads, hd)))
                outs["sbv"][n].append(shaped(vf, n, (n_heads, hd)))
                if n == "p":
                    y = _sb_prompt(qb, kb, vb, batch, seq, n_heads, hd)
                else:
                    y = _rows_by_query(_paged_attention(
                        "sb", page_table, j, _rows_by_class(qb, db, ds, 1, n_heads),
                        _pad_rows(kf.reshape(db, ds, n_heads, hd), page),
                        _pad_rows(vf.reshape(db, ds, n_heads, hd), page),
                        cache_sb_k, cache_sb_v, n_pages), db, ds)
                w_out = w_b_out
            elif kind == 2:
                lambda_init = 0.8 - 0.6 * math.exp(-0.3 * i)
                w = w_c_qkv
                rope = rope_hd[n] + (hd // 2, rope_blocks[n])
                (qb,) = _proj(xb, w, j, 0, d, (BF16,), tm, rope)
                kf, kb = _proj(xb, w, j, d, d, (F32, BF16), tm, rope)
                vf, vb = _proj(xb, w, j, 2 * d, d, (F32, BF16), tm)
                outs["dk"][n].append(shaped(kf, n, (n_dh, 2, hd)))
                outs["dv"][n].append(shaped(vf, n, (n_dh, 2 * hd)))
                if n == "p":
                    y = _diff_prompt(qb, kb, vb, lam_c[j], g_c_subln[j], batch, seq, n_dh, hd, lambda_init)
                else:
                    y = _rows_by_query(_paged_attention(
                        "diff", page_table, j, _rows_by_class(qb, db, ds, 2, n_dh),
                        _pad_rows(kf.reshape(db, ds, n_heads, hd), page),
                        _pad_rows(vf.reshape(db, ds, n_dh, 2 * hd), page),
                        cache_diff_k.reshape(cache_diff_k.shape[:3] + (n_heads, hd)), cache_diff_v, n_pages,
                        lam=lam_c[j], g_sub=g_c_subln[j], lambda_init=lambda_init), db, ds)
                w_out = w_c_out
            else:
                w = w_d_in
                rope = rope_hd[n] + (hd // 2, rope_blocks[n])
                (qb,) = _proj(xb, w, j, 0, d, (BF16,), tm, rope)
                kf, kb = _proj(xb, w, j, d, d, (F32, BF16), tm, rope)
                vf, vb = _proj(xb, w, j, 2 * d, d, (F32, BF16), tm)
                (iqb,) = _proj(xb, w, j, 3 * d, n_ih * idim, (BF16,), tm, rope_idx[n] + (idim // 2, rope_blocks[n]))
                o2 = 3 * d + n_ih * idim
                w_ik, w_iw = w[j:j + 1, :, o2:o2 + idim], w[j:j + 1, :, o2 + idim:]
                w_tail = jnp.concatenate([w_ik, w_iw, jnp.zeros((1, d, LANES - idim - n_ih), F32)], axis=2)
                (tail,) = _proj(xb, w_tail, 0, 0, LANES, (F32,), tm, rope_tail[n] + (idim // 2, rope_blocks[n]))
                outs["ak"][n].append(shaped(kf, n, (n_heads, hd)))
                outs["av"][n].append(shaped(vf, n, (n_heads, hd)))
                outs["ai"][n].append(shaped(tail[:, :idim], n, (idim,)))
                if n == "p":
                    w_ik2 = jnp.concatenate([w_ik] * (LANES // idim), axis=2)
                    (ik2b,) = _proj(xb, w_ik2, 0, 0, LANES, (BF16,), tm, rope_idx[n] + (idim // 2, rope_blocks[n]))
                    sel = _select_prompt(iqb, ik2b, tail, batch, seq, n_ih, idim, k_top[n])
                    y = _dsa_prompt(qb, kb, vb, sel, batch, seq, n_heads, hd)
                else:
                    qm = iqb.reshape(db, ds, n_ih, idim).transpose(0, 2, 1, 3).reshape(db, n_ih * ds, idim)
                    wcol = tail[:, idim:idim + n_ih].reshape(db, ds, n_ih).transpose(0, 2, 1).reshape(db, n_ih * ds, 1)
                    wcol = wcol * ((idim ** -0.5) * (n_ih ** -0.5))
                    iknew = _pad_rows(tail[:, :idim].reshape(db, ds, idim), page)
                    sel = _select_sample(page_table, qm, wcol, iknew, cache_dsa_idx_k, j, n_pages, k_top[n])
                    y = _rows_by_query(_paged_attention(
                        "dsa", page_table, j, _rows_by_class(qb, db, ds, 1, n_heads),
                        _pad_rows(kf.reshape(db, ds, n_heads, hd), page),
                        _pad_rows(vf.reshape(db, ds, n_heads, hd), page),
                        cache_dsa_k, cache_dsa_v, n_pages, sel=sel), db, ds)
                w_out = w_d_out
            x[n] = _mm_res(y, w_out, j, x[n], tm)
            fb = _rmsnorm(x[n], g_ffn[i], BF16, tm)
            hist = None if n == "p" else _expand_history(state_ffn_conv[i])
            tm_up = 2 * ROW_BLOCK if (n == "p" and slen % (2 * ROW_BLOCK) == 0) else tm
            act, st = _ffn_up(fb, w_ffn_up, conv_ffn, i, hist, tm_up, slen)
            outs["fc"][n].append(_last_two(st, nseq[n]))
            x[n] = _mm_res(act, w_ffn_down, i, x[n], tm)

    y_prompt = _rmsnorm(x["p"], g_final, F32, tm_p).reshape(batch, seq, d)
    y_sample = _rmsnorm(x["s"], g_final, F32, ms).reshape(db, ds, d)
    res = [y_prompt, y_sample]
    for name in ("conv", "sbk", "sbv", "dk", "dv", "ak", "av", "ai", "fc"):
        for n in ("p", "s"):
            res.append(jnp.stack(outs[name][n]))
    return tuple(res)
```

```python
import functools
import math

import jax
import jax.numpy as jnp
from jax import lax
from jax.experimental import pallas as pl
from jax.experimental.pallas import tpu as pltpu

NORM_EPS = 1e-6
ROPE_THETA = 10000.0
TOPK_MAX = 256
CONV_W = 3
LANES = 128
SUBLANES = 8
MXU_WIDTH = 256
FLASH_BLOCK = 512
ROW_BLOCK = 512
W_TILE_BYTES = 12 * 1024 * 1024
SB_SKIP_BELOW = -104.0
VMEM_LIMIT_BYTES = 56 * 1024 * 1024
NEG_BIG = -1e30
M_INIT = -1e29
INT_MIN = -2 ** 31
LOG2E = math.log2(math.e)

F32 = jnp.float32
BF16 = jnp.bfloat16


def _params(*sem):
    return pltpu.CompilerParams(dimension_semantics=sem, vmem_limit_bytes=VMEM_LIMIT_BYTES)


def _dot(a, b):
    return jnp.dot(a, b, preferred_element_type=F32)


def _dot_t(a, b):
    return lax.dot_general(a, b, (((1,), (1,)), ((), ())), preferred_element_type=F32)


def _col_chunk(tn):
    return MXU_WIDTH if tn % MXU_WIDTH == 0 else LANES


def _rmsnorm_kernel(x_ref, g_ref, o_ref):
    x = x_ref[...]
    ms = jnp.mean(x * x, axis=-1, keepdims=True)
    o_ref[...] = ((x * lax.rsqrt(ms + NORM_EPS)) * g_ref[...]).astype(o_ref.dtype)


def _rmsnorm(x, g, out_dtype, tm):
    m, d = x.shape
    return pl.pallas_call(
        _rmsnorm_kernel,
        grid=(m // tm,),
        in_specs=[pl.BlockSpec((tm, d), lambda i: (i, 0)), pl.BlockSpec((1, d), lambda i: (0, 0))],
        out_specs=pl.BlockSpec((tm, d), lambda i: (i, 0)),
        out_shape=jax.ShapeDtypeStruct((m, d), out_dtype),
        compiler_params=_params("arbitrary"),
    )(x, g.reshape(1, d))


def _rope_group(y, c, s, half):
    if 2 * half == LANES:
        rot = pltpu.roll(y, half, 1)
    else:
        lane = lax.broadcasted_iota(jnp.int32, y.shape, 1)
        rot = jnp.where(lane % (2 * half) < half, pltpu.roll(y, LANES - half, 1), pltpu.roll(y, half, 1))
    return y * c + rot * s


def _proj_kernel(*refs, half, n_out):
    if half:
        x_ref, w_ref, c_ref, s_ref = refs[:4]
        outs = refs[4:4 + n_out]
    else:
        x_ref, w_ref = refs[:2]
        outs = refs[2:2 + n_out]
    wb_ref = refs[-1]

    @pl.when(pl.program_id(1) == 0)
    def _():
        wb_ref[...] = w_ref[...].astype(BF16)

    x = x_ref[...]
    tn = wb_ref.shape[1]
    ch = _col_chunk(tn)
    for c0 in range(0, tn, ch):
        y = _dot(x, wb_ref[:, c0:c0 + ch])
        if half:
            c = c_ref[...]
            s = s_ref[...]
            for g in range(ch // LANES):
                sl = slice(c0 + g * LANES, c0 + (g + 1) * LANES)
                yg = _rope_group(y[:, g * LANES:(g + 1) * LANES], c, s, half)
                for o in outs:
                    o[:, sl] = yg.astype(o.dtype)
        else:
            for o in outs:
                o[:, c0:c0 + ch] = y.astype(o.dtype)


def _proj(xb, w, layer, col0, ncols, out_dtypes, tm, rope=None):
    m, d = xb.shape
    tn = min(1024, ncols)
    assert ncols % tn == 0 and col0 % tn == 0 and m % tm == 0
    off = col0 // tn
    in_specs = [pl.BlockSpec((tm, d), lambda j, i: (i, 0)),
                pl.BlockSpec((None, d, tn), lambda j, i: (layer, 0, j + off))]
    args = [xb, w]
    half = 0
    if rope is not None:
        ctab, stab, half, tblocks = rope
        in_specs += [pl.BlockSpec((tm, LANES), lambda j, i: (i % tblocks, 0)),
                     pl.BlockSpec((tm, LANES), lambda j, i: (i % tblocks, 0))]
        args += [ctab, stab]
    outs = pl.pallas_call(
        functools.partial(_proj_kernel, half=half, n_out=len(out_dtypes)),
        grid=(ncols // tn, m // tm),
        in_specs=in_specs,
        out_specs=[pl.BlockSpec((tm, tn), lambda j, i: (i, j)) for _ in out_dtypes],
        out_shape=[jax.ShapeDtypeStruct((m, ncols), dt) for dt in out_dtypes],
        scratch_shapes=[pltpu.VMEM((d, tn), BF16)],
        compiler_params=_params("arbitrary", "arbitrary"),
    )(*args)
    return outs


def _mm_res_kernel(a_ref, w_ref, r_ref, o_ref, wb_ref):
    @pl.when(pl.program_id(1) == 0)
    def _():
        wb_ref[...] = w_ref[...].astype(BF16)

    o_ref[...] = r_ref[...] + _dot(a_ref[...], wb_ref[...])


def _mm_res(a, w, layer, res, tm):
    m, k = a.shape
    n = w.shape[2]
    tn = min(1024 if k * 1024 * 4 <= W_TILE_BYTES else 512, n)
    return pl.pallas_call(
        _mm_res_kernel,
        grid=(n // tn, m // tm),
        in_specs=[pl.BlockSpec((tm, k), lambda j, i: (i, 0)),
                  pl.BlockSpec((None, k, tn), lambda j, i: (layer, 0, j)),
                  pl.BlockSpec((tm, tn), lambda j, i: (i, j))],
        out_specs=pl.BlockSpec((tm, tn), lambda j, i: (i, j)),
        out_shape=jax.ShapeDtypeStruct((m, n), F32),
        scratch_shapes=[pltpu.VMEM((k, tn), BF16)],
        compiler_params=_params("arbitrary", "arbitrary"),
    )(a, w, res)


def _shifted(u, prev, period):
    row = lax.broadcasted_iota(jnp.int32, u.shape, 0)
    if period == 0:
        p1 = prev[SUBLANES - 1:SUBLANES, :]
        p2 = prev[SUBLANES - 2:SUBLANES - 1, :]
        u1 = jnp.where(row >= 1, pltpu.roll(u, 1, 0), p1)
        u2 = jnp.where(row >= 2, pltpu.roll(u, 2, 0), jnp.where(row == 1, p1, p2))
        return u1, u2
    row = row % period
    u1 = jnp.where(row >= 1, pltpu.roll(u, 1, 0), pltpu.roll(prev, 1, 0))
    u2 = jnp.where(row >= 2, pltpu.roll(u, 2, 0), pltpu.roll(prev, 2, 0))
    return u1, u2


def _dwconv(u, u1, u2, cw):
    return (cw[0:1, :] * u2 + cw[1:2, :] * u1) + cw[2:3, :] * u


def _history(prev_ref, hist_ref, tiles_per_seq):
    if hist_ref is not None:
        return hist_ref[...]

    @pl.when(pl.program_id(1) % tiles_per_seq == 0)
    def _():
        prev_ref[...] = jnp.zeros_like(prev_ref)

    return prev_ref[...]


def _convmix_kernel(*refs, short, tiles_per_seq, period):
    if short:
        x_ref, wb_ref, wc_ref, wh_ref, cw_ref, hist_ref, y_ref, st_ref, wbb, wcb, whb, prev_ref = refs
    else:
        x_ref, wb_ref, wc_ref, wh_ref, cw_ref, y_ref, st_ref, wbb, wcb, whb, prev_ref = refs
        hist_ref = None

    @pl.when(pl.program_id(1) == 0)
    def _():
        wbb[...] = wb_ref[...].astype(BF16)
        wcb[...] = wc_ref[...].astype(BF16)
        whb[...] = wh_ref[...].astype(BF16)

    x = x_ref[...]
    prev_all = _history(prev_ref, hist_ref, tiles_per_seq)
    tn = wbb.shape[1]
    ch = _col_chunk(tn)
    for c0 in range(0, tn, ch):
        sl = slice(c0, c0 + ch)
        u = _dot(x, wcb[:, sl]) * _dot(x, whb[:, sl])
        u1, u2 = _shifted(u, prev_all[:, sl], period)
        conv = _dwconv(u, u1, u2, cw_ref[:, sl])
        y_ref[:, sl] = (_dot(x, wbb[:, sl]) * conv).astype(y_ref.dtype)
        if short:
            st_ref[:, sl] = u
        else:
            prev_ref[:, sl] = u[u.shape[0] - SUBLANES:, :]
            st_ref[:, sl] = u[u.shape[0] - SUBLANES:, :]


def _state_specs(short, tm, tn, tiles_per_seq, col_off=0):
    if short:
        return pl.BlockSpec((tm, tn), lambda j, i: (0, j + col_off))
    return pl.BlockSpec((SUBLANES, tn), lambda j, i: (i // tiles_per_seq, j + col_off))


def _seq_mode(hist_x, m, tm, seq_len):
    if hist_x is not None:
        assert seq_len == SUBLANES and tm == m
        return True, 1, seq_len, m // seq_len, tm
    assert seq_len % tm == 0
    return False, seq_len // tm, 0, m // seq_len, SUBLANES


def _convmix(xb, w_in, conv_w, layer, hist_x, tm, seq_len):
    m, d = xb.shape
    tn = min(512, d)
    nj = d // tn
    short, tiles_per_seq, period, nseq, carry_rows = _seq_mode(hist_x, m, tm, seq_len)
    in_specs = [pl.BlockSpec((tm, d), lambda j, i: (i, 0)),
                pl.BlockSpec((None, d, tn), lambda j, i: (layer, 0, j)),
                pl.BlockSpec((None, d, tn), lambda j, i: (layer, 0, j + nj)),
                pl.BlockSpec((None, d, tn), lambda j, i: (layer, 0, j + 2 * nj)),
                pl.BlockSpec((None, CONV_W, tn), lambda j, i: (layer, 0, j))]
    args = [xb, w_in, w_in, w_in, conv_w]
    if short:
        in_specs.append(pl.BlockSpec((tm, tn), lambda j, i: (0, j)))
        args.append(hist_x)
    y, st = pl.pallas_call(
        functools.partial(_convmix_kernel, short=short, tiles_per_seq=tiles_per_seq, period=period),
        grid=(nj, m // tm),
        in_specs=in_specs,
        out_specs=[pl.BlockSpec((tm, tn), lambda j, i: (i, j)), _state_specs(short, tm, tn, tiles_per_seq)],
        out_shape=[jax.ShapeDtypeStruct((m, d), BF16), jax.ShapeDtypeStruct((nseq * SUBLANES, d), F32)],
        scratch_shapes=[pltpu.VMEM((d, tn), BF16)] * 3 + [pltpu.VMEM((carry_rows, tn), F32)],
        compiler_params=_params("arbitrary", "arbitrary"),
    )(*args)
    return y, st


def _ffn_up_kernel(*refs, short, tiles_per_seq, period):
    if short:
        (x_ref, wg_ref, wu_ref, cg_ref, cu_ref, hg_ref, hu_ref,
         a_ref, sg_ref, su_ref, wgb, wub, pg_ref, pu_ref) = refs
    else:
        x_ref, wg_ref, wu_ref, cg_ref, cu_ref, a_ref, sg_ref, su_ref, wgb, wub, pg_ref, pu_ref = refs
        hg_ref = hu_ref = None

    @pl.when(pl.program_id(1) == 0)
    def _():
        wgb[...] = wg_ref[...].astype(BF16)
        wub[...] = wu_ref[...].astype(BF16)

    prevs = (_history(pg_ref, hg_ref, tiles_per_seq), _history(pu_ref, hu_ref, tiles_per_seq))
    tm = x_ref.shape[0]
    rb = min(tm, ROW_BLOCK)
    tn = wgb.shape[1]
    ch = _col_chunk(tn)
    for c0 in range(0, tn, ch):
        sl = slice(c0, c0 + ch)
        prev = [prevs[0][:, sl], prevs[1][:, sl]]
        for r0 in range(0, tm, rb):
            x = x_ref[r0:r0 + rb, :]
            halves = []
            for h, (w_b, c_ref, s_ref) in enumerate(((wgb, cg_ref, sg_ref), (wub, cu_ref, su_ref))):
                up = _dot(x, w_b[:, sl])
                u1, u2 = _shifted(up, prev[h], period)
                halves.append(_dwconv(up, u1, u2, c_ref[:, sl]))
                if short:
                    s_ref[:, sl] = up
                else:
                    prev[h] = up[rb - SUBLANES:, :]
            g, u = halves
            a_ref[r0:r0 + rb, sl] = ((g * jax.nn.sigmoid(g)) * u).astype(a_ref.dtype)
        if not short:
            for h, (p_ref, s_ref) in enumerate(((pg_ref, sg_ref), (pu_ref, su_ref))):
                p_ref[:, sl] = prev[h]
                s_ref[:, sl] = prev[h]


def _ffn_up(xb, w_up, conv_w, layer, hist_x, tm, seq_len):
    m, d = xb.shape
    d_ff = w_up.shape[2] // 2
    tn = 512 if d_ff % 512 == 0 else LANES
    nj = d_ff // tn
    short, tiles_per_seq, period, nseq, carry_rows = _seq_mode(hist_x, m, tm, seq_len)
    in_specs = [pl.BlockSpec((tm, d), lambda j, i: (i, 0)),
                pl.BlockSpec((None, d, tn), lambda j, i: (layer, 0, j)),
                pl.BlockSpec((None, d, tn), lambda j, i: (layer, 0, j + nj)),
                pl.BlockSpec((None, CONV_W, tn), lambda j, i: (layer, 0, j)),
                pl.BlockSpec((None, CONV_W, tn), lambda j, i: (layer, 0, j + nj))]
    args = [xb, w_up, w_up, conv_w, conv_w]
    if short:
        in_specs += [pl.BlockSpec((tm, tn), lambda j, i: (0, j)), pl.BlockSpec((tm, tn), lambda j, i: (0, j + nj))]
        args += [hist_x, hist_x]
    act, sg, su = pl.pallas_call(
        functools.partial(_ffn_up_kernel, short=short, tiles_per_seq=tiles_per_seq, period=period),
        grid=(nj, m // tm),
        in_specs=in_specs,
        out_specs=[pl.BlockSpec((tm, tn), lambda j, i: (i, j)),
                   _state_specs(short, tm, tn, tiles_per_seq),
                   _state_specs(short, tm, tn, tiles_per_seq)],
        out_shape=[jax.ShapeDtypeStruct((m, d_ff), BF16),
                   jax.ShapeDtypeStruct((nseq * SUBLANES, d_ff), F32),
                   jax.ShapeDtypeStruct((nseq * SUBLANES, d_ff), F32)],
        scratch_shapes=[pltpu.VMEM((d, tn), BF16)] * 2 + [pltpu.VMEM((carry_rows, tn), F32)] * 2,
        compiler_params=_params("arbitrary", "arbitrary"),
    )(*args)
    return act, jnp.concatenate([sg, su], axis=1)


def _log_sigmoid_pair(z):
    l1p = jnp.log(1.0 + jnp.exp(-jnp.abs(z)))
    ls = jnp.minimum(z, 0.0) - l1p
    return ls, ls - z


def _suffix_sum(x, tri):
    hi = x.astype(BF16)
    lo = (x - hi.astype(F32)).astype(BF16)
    return _dot(hi, tri) + _dot(lo, tri)


def _tri(n):
    j = lax.broadcasted_iota(jnp.int32, (n, n), 0)
    s = lax.broadcasted_iota(jnp.int32, (n, n), 1)
    return (j > s).astype(BF16)


def _sb_prompt_kernel(q_ref, k_ref, v_ref, tri_ref, o_ref, *, tq, hd, scale):
    qi = pl.program_id(2)
    tri = tri_ref[...]
    n_h = q_ref.shape[1] // hd

    def block(ki, h, carry, acc, diag):
        start = pl.multiple_of(ki * tq, tq)
        hs = slice(h * hd, (h + 1) * hd)
        k = k_ref[pl.ds(start, tq), hs]
        v = v_ref[pl.ds(start, tq), hs]
        z = _dot_t(q_ref[:, hs], k) * scale
        ls, l1m = _log_sigmoid_pair(z)
        if diag:
            row = lax.broadcasted_iota(jnp.int32, z.shape, 0)
            col = lax.broadcasted_iota(jnp.int32, z.shape, 1)
            valid = col < row
            l1m = jnp.where(valid, l1m, 0.0)
        after = _suffix_sum(l1m, tri)
        a = jnp.exp(ls + after + carry)
        if diag:
            a = jnp.where(valid, a, 0.0)
        acc = acc + _dot(a.astype(BF16), v)
        carry = carry + jnp.sum(l1m, axis=1, keepdims=True)
        return carry, acc

    state = tuple(block(qi, h, jnp.zeros((tq, 1), F32), jnp.zeros((tq, hd), F32), True) for h in range(n_h))

    def more(c):
        top = c[1][0][0]
        for h in range(1, n_h):
            top = jnp.maximum(top, c[1][h][0])
        return (c[0] < qi) & (jnp.max(top) > SB_SKIP_BELOW)

    def body(c):
        return c[0] + 1, tuple(block(qi - 1 - c[0], h, c[1][h][0], c[1][h][1], False) for h in range(n_h))

    _, state = lax.while_loop(more, body, (jnp.int32(0), state))
    for h in range(n_h):
        o_ref[:, h * hd:(h + 1) * hd] = state[h][1].astype(o_ref.dtype)


def _sb_prompt(qb, kb, vb, batch, seq, n_heads, hd):
    m, d = qb.shape
    tq = min(256, seq)
    nq = seq // tq
    n_h = 2 if n_heads % 2 == 0 else 1
    w = n_h * hd
    return pl.pallas_call(
        functools.partial(_sb_prompt_kernel, tq=tq, hd=hd, scale=hd ** -0.5),
        grid=(batch, n_heads // n_h, nq),
        in_specs=[pl.BlockSpec((tq, w), lambda b, h, q: (b * nq + q, h)),
                  pl.BlockSpec((seq, w), lambda b, h, q: (b, h)),
                  pl.BlockSpec((seq, w), lambda b, h, q: (b, h)),
                  pl.BlockSpec((tq, tq), lambda b, h, q: (0, 0))],
        out_specs=pl.BlockSpec((tq, w), lambda b, h, q: (b * nq + q, h)),
        out_shape=jax.ShapeDtypeStruct((m, d), BF16),
        compiler_params=_params("arbitrary", "arbitrary", "arbitrary"),
    )(qb, kb, vb, _tri(tq))


def _online_softmax_step(s, mask, m, l, acc, v):
    if mask is not None:
        s = jnp.where(mask, s, NEG_BIG)
    m_new = jnp.maximum(m, jnp.max(s, axis=1, keepdims=True))
    p = jnp.exp2(s - m_new)
    if mask is not None:
        p = jnp.where(mask, p, 0.0)
    alpha = jnp.exp2(m - m_new)
    l = alpha * l + jnp.sum(p, axis=1, keepdims=True)
    acc = alpha * acc + _dot(p.astype(BF16), v)
    return m_new, l, acc


def _lambda_full(lam_ref, lambda_init):
    lf = lam_ref[...]
    s01 = jnp.sum(lf[0:1, :] * lf[1:2, :], axis=1, keepdims=True)
    s23 = jnp.sum(lf[2:3, :] * lf[3:4, :], axis=1, keepdims=True)
    return jnp.exp(s01) - jnp.exp(s23) + lambda_init


def _sub_norm(o, g, lambda_init):
    ms = jnp.mean(o * o, axis=-1, keepdims=True)
    return ((o * lax.rsqrt(ms + NORM_EPS)) * g) * (1.0 - lambda_init)


def _diff_prompt_kernel(q_ref, k_ref, v_ref, lam_ref, g_ref, o_ref, *, tq, hd, scale, lambda_init):
    qi = pl.program_id(2)
    dv = v_ref.shape[1]

    def block(ki, state, diag):
        start = pl.multiple_of(ki * tq, tq)
        v = v_ref[pl.ds(start, tq), :]
        mask = None
        if diag:
            row = lax.broadcasted_iota(jnp.int32, (tq, tq), 0)
            col = lax.broadcasted_iota(jnp.int32, (tq, tq), 1)
            mask = col <= row
        new = []
        for i in range(2):
            q = q_ref[:, i * hd:(i + 1) * hd]
            k = k_ref[pl.ds(start, tq), i * hd:(i + 1) * hd]
            s = _dot_t(q, k) * (scale * LOG2E)
            new.append(_online_softmax_step(s, mask, *state[i], v))
        return tuple(new)

    init = tuple((jnp.full((tq, 1), NEG_BIG, F32), jnp.zeros((tq, 1), F32), jnp.zeros((tq, dv), F32))
                 for _ in range(2))
    state = lax.fori_loop(0, qi, lambda ki, st: block(ki, st, False), init)
    state = block(qi, state, True)
    lam = _lambda_full(lam_ref, lambda_init)
    (_, l0, a0), (_, l1, a1) = state
    o = a0 / l0 - lam * (a1 / l1)
    o_ref[...] = _sub_norm(o, g_ref[...], lambda_init).astype(o_ref.dtype)


def _diff_prompt(qb, kb, vb, lam, g_sub, batch, seq, n_dh, hd, lambda_init):
    m, d = qb.shape
    tq = min(FLASH_BLOCK, seq)
    nq = seq // tq
    w = 2 * hd
    return pl.pallas_call(
        functools.partial(_diff_prompt_kernel, tq=tq, hd=hd, scale=hd ** -0.5, lambda_init=lambda_init),
        grid=(batch, n_dh, nq),
        in_specs=[pl.BlockSpec((tq, w), lambda b, h, q: (b * nq + q, h)),
                  pl.BlockSpec((seq, w), lambda b, h, q: (b, h)),
                  pl.BlockSpec((seq, w), lambda b, h, q: (b, h)),
                  pl.BlockSpec(lam.shape, lambda b, h, q: (0, 0)),
                  pl.BlockSpec((1, w), lambda b, h, q: (0, 0))],
        out_specs=pl.BlockSpec((tq, w), lambda b, h, q: (b * nq + q, h)),
        out_shape=jax.ShapeDtypeStruct((m, d), BF16),
        compiler_params=_params("arbitrary", "arbitrary", "arbitrary"),
    )(qb, kb, vb, lam, g_sub.reshape(1, w))


def _dsa_prompt_kernel(q_ref, k_ref, v_ref, sel_ref, o_ref, *, tq, hd, scale):
    qi = pl.program_id(2)
    n_h = q_ref.shape[1] // hd

    def block(ki, state):
        start = pl.multiple_of(ki * tq, tq)
        bias = sel_ref[:, pl.ds(start, tq)].astype(F32)
        new = []
        for h in range(n_h):
            hs = slice(h * hd, (h + 1) * hd)
            s = _dot_t(q_ref[:, hs], k_ref[pl.ds(start, tq), hs]) * (scale * LOG2E) + bias
            new.append(_online_softmax_step(s, None, *state[h], v_ref[pl.ds(start, tq), hs]))
        return tuple(new)

    init = tuple((jnp.full((tq, 1), M_INIT, F32), jnp.zeros((tq, 1), F32), jnp.zeros((tq, hd), F32))
                 for _ in range(n_h))
    state = lax.fori_loop(0, qi + 1, block, init)
    for h in range(n_h):
        _, l, acc = state[h]
        o_ref[:, h * hd:(h + 1) * hd] = (acc / l).astype(o_ref.dtype)


def _dsa_prompt(qb, kb, vb, sel, batch, seq, n_heads, hd):
    m, d = qb.shape
    tq = min(FLASH_BLOCK, seq)
    nq = seq // tq
    n_h = 2 if n_heads % 2 == 0 else 1
    w = n_h * hd
    return pl.pallas_call(
        functools.partial(_dsa_prompt_kernel, tq=tq, hd=hd, scale=hd ** -0.5),
        grid=(batch, n_heads // n_h, nq),
        in_specs=[pl.BlockSpec((tq, w), lambda b, h, q: (b * nq + q, h)),
                  pl.BlockSpec((seq, w), lambda b, h, q: (b, h)),
                  pl.BlockSpec((seq, w), lambda b, h, q: (b, h)),
                  pl.BlockSpec((tq, seq), lambda b, h, q: (b * nq + q, 0))],
        out_specs=pl.BlockSpec((tq, w), lambda b, h, q: (b * nq + q, h)),
        out_shape=jax.ShapeDtypeStruct((m, d), BF16),
        compiler_params=_params("arbitrary", "arbitrary", "arbitrary"),
    )(qb, kb, vb, sel)


def _order_key(x):
    bits = pltpu.bitcast(x + 0.0, jnp.int32)
    return jnp.where(bits >= 0, bits, bits ^ jnp.int32(0x7FFFFFFF))


def _kth_largest_key(count_ge, rows, k):
    zero = jnp.zeros((rows, 1), jnp.int32)
    c0 = count_ge(zero)
    t0 = jnp.where(c0 >= k, zero, jnp.int32(INT_MIN))
    open0 = (c0 != k).astype(jnp.int32)

    def more(c):
        return (c[0] < 31) & (jnp.max(c[2]) > 0)

    def body(c):
        it, t, still = c
        cand = t | jnp.left_shift(jnp.int32(1), jnp.int32(30) - it)
        n = count_ge(cand)
        take = (n >= k) & (still > 0)
        return it + 1, jnp.where(take, cand, t), jnp.where(n == k, 0, still)

    return lax.while_loop(more, body, (jnp.int32(0), t0, open0))[1]


def _store_topk_mask(o_ref, key, valid, k, t):
    ge = valid & (key >= t)
    n_ge = jnp.sum(ge.astype(jnp.int32), axis=1, keepdims=True)
    has_ties = jnp.max(n_ge) > k

    @pl.when(jnp.logical_not(has_ties))
    def _():
        o_ref[...] = jnp.where(ge, 0.0, NEG_BIG).astype(o_ref.dtype)

    @pl.when(has_ties)
    def _():
        gt = key > t
        eq = key == t
        need = k - jnp.sum(gt.astype(jnp.int32), axis=1, keepdims=True)
        col = lax.broadcasted_iota(jnp.int32, key.shape, 1)
        nbits = int(key.shape[1]).bit_length()

        def body(it, j):
            cand = j | jnp.left_shift(jnp.int32(1), jnp.int32(nbits - 1) - it)
            c = jnp.sum((eq & (col < cand)).astype(jnp.int32), axis=1, keepdims=True)
            return jnp.where(c < need, cand, j)

        j = lax.fori_loop(0, nbits, body, jnp.zeros_like(t))
        sel = valid & (gt | (eq & (col <= j)))
        o_ref[...] = jnp.where(sel, 0.0, NEG_BIG).astype(o_ref.dtype)


def _select_prompt_kernel(iq_ref, ik_ref, iw_ref, o_ref, key_ref, qh_ref, *, tq, cw, n_ih, idim, k_top, wscale):
    qi = pl.program_id(1)
    seq = key_ref.shape[1]
    n_need = (qi * tq) // cw + 1
    iw = iw_ref[...] * wscale
    lane = lax.broadcasted_iota(jnp.int32, (tq, LANES), 1)
    per_group = LANES // idim
    for h in range(n_ih):
        g, r = divmod(h, per_group)
        qg = iq_ref[:, g * LANES:(g + 1) * LANES].astype(F32)
        qh_ref[h] = jnp.where((lane >= r * idim) & (lane < (r + 1) * idim), qg, 0.0).astype(BF16)

    def chunk_at(c):
        return pl.ds(pl.multiple_of(c * cw, cw), cw)

    def score_chunk(c, carry):
        ikc = ik_ref[chunk_at(c), :]
        acc = None
        for h in range(n_ih):
            s = jnp.maximum(_dot_t(qh_ref[h], ikc), 0.0) * iw[:, idim + h:idim + h + 1]
            acc = s if acc is None else acc + s
        row = qi * tq + lax.broadcasted_iota(jnp.int32, (tq, cw), 0)
        col = c * cw + lax.broadcasted_iota(jnp.int32, (tq, cw), 1)
        key_ref[:, chunk_at(c)] = _order_key(jnp.where(col <= row, acc, -jnp.inf))
        return carry

    def fill_chunk(c, carry):
        key_ref[:, chunk_at(c)] = jnp.full((tq, cw), INT_MIN, jnp.int32)
        return carry

    lax.fori_loop(0, n_need, score_chunk, 0)
    lax.fori_loop(n_need, seq // cw, fill_chunk, 0)

    def count_ge(t):
        def body(c, acc):
            m = (key_ref[:, chunk_at(c)] >= t).astype(jnp.int32)
            for v in range(cw // LANES):
                acc = acc + m[:, v * LANES:(v + 1) * LANES]
            return acc

        acc = lax.fori_loop(0, n_need, body, jnp.zeros((tq, LANES), jnp.int32))
        return jnp.sum(acc, axis=1, keepdims=True)

    t = _kth_largest_key(count_ge, tq, k_top)
    row = qi * tq + lax.broadcasted_iota(jnp.int32, (tq, seq), 0)
    col = lax.broadcasted_iota(jnp.int32, (tq, seq), 1)
    _store_topk_mask(o_ref, key_ref[...], col <= row, k_top, t)


def _select_prompt(iqb, ik2b, tail, batch, seq, n_ih, idim, k_top):
    m = iqb.shape[0]
    tq = min(128, seq)
    cw = min(512, seq)
    nq = seq // tq
    assert LANES % idim == 0 and cw % tq == 0 and seq % cw == 0
    return pl.pallas_call(
        functools.partial(_select_prompt_kernel, tq=tq, cw=cw, n_ih=n_ih, idim=idim, k_top=k_top,
                          wscale=(idim ** -0.5) * (n_ih ** -0.5)),
        grid=(batch, nq),
        in_specs=[pl.BlockSpec((tq, n_ih * idim), lambda b, q: (b * nq + q, 0)),
                  pl.BlockSpec((seq, LANES), lambda b, q: (b, 0)),
                  pl.BlockSpec((tq, LANES), lambda b, q: (b * nq + q, 0))],
        out_specs=pl.BlockSpec((tq, seq), lambda b, q: (b * nq + q, 0)),
        out_shape=jax.ShapeDtypeStruct((m, seq), BF16),
        scratch_shapes=[pltpu.VMEM((tq, seq), jnp.int32), pltpu.VMEM((n_ih, tq, LANES), BF16)],
        compiler_params=_params("arbitrary", "arbitrary"),
    )(iqb, ik2b, tail)


def _sel_sample_kernel(pt_ref, qm_ref, w_ref, iknew_ref, *refs, pages_per_step, n_pages, ds, k_top):
    ik_refs = refs[:pages_per_step]
    o_ref, sc_ref = refs[pages_per_step:pages_per_step + 2]
    s_id = pl.program_id(1)
    qm = qm_ref[...]
    w = w_ref[...]
    page = ik_refs[0].shape[0]

    def scores(ik):
        s = jnp.maximum(_dot_t(qm, ik.astype(BF16)), 0.0) * w
        tot = s[0:ds, :]
        for h in range(1, s.shape[0] // ds):
            tot = tot + s[h * ds:(h + 1) * ds, :]
        return tot

    for c in range(pages_per_step):
        pg = s_id * pages_per_step + c
        sc_ref[:, pl.ds(pl.multiple_of(pg * page, page), page)] = scores(ik_refs[c][...])

    @pl.when(s_id == pl.num_programs(1) - 1)
    def _():
        t = lax.broadcasted_iota(jnp.int32, (ds, page), 0)
        j = lax.broadcasted_iota(jnp.int32, (ds, page), 1)
        sc_ref[:, n_pages * page:] = jnp.where(j <= t, scores(iknew_ref[...]), -jnp.inf)
        col = lax.broadcasted_iota(jnp.int32, sc_ref.shape, 1)
        row = lax.broadcasted_iota(jnp.int32, sc_ref.shape, 0)
        key = _order_key(sc_ref[...])

        def count_ge(thr):
            return jnp.sum((key >= thr).astype(jnp.int32), axis=1, keepdims=True)

        thr = _kth_largest_key(count_ge, ds, k_top)
        _store_topk_mask(o_ref, key, col <= n_pages * page + row, k_top, thr)


def _select_sample(page_table, qm, wcol, iknew, pool, layer, n_pages, k_top):
    db, rows, idim = qm.shape
    page = pool.shape[2]
    ds = SUBLANES
    pps = 8 if n_pages % 8 == 0 else (4 if n_pages % 4 == 0 else 1)
    n_steps = n_pages // pps
    nk = (n_pages + 1) * page

    def pool_spec(c):
        return pl.BlockSpec((None, None, page, idim), lambda b, s, pt: (layer, pt[b, s * pps + c], 0, 0))

    grid_spec = pltpu.PrefetchScalarGridSpec(
        num_scalar_prefetch=1,
        grid=(db, n_steps),
        in_specs=[pl.BlockSpec((None, rows, idim), lambda b, s, pt: (b, 0, 0)),
                  pl.BlockSpec((None, rows, 1), lambda b, s, pt: (b, 0, 0)),
                  pl.BlockSpec((None, page, idim), lambda b, s, pt: (b, 0, 0))]
                 + [pool_spec(c) for c in range(pps)],
        out_specs=pl.BlockSpec((None, ds, nk), lambda b, s, pt: (b, 0, 0)),
        scratch_shapes=[pltpu.VMEM((ds, nk), F32)],
    )
    return pl.pallas_call(
        functools.partial(_sel_sample_kernel, pages_per_step=pps, n_pages=n_pages, ds=ds, k_top=k_top),
        grid_spec=grid_spec,
        out_shape=jax.ShapeDtypeStruct((db, ds, nk), BF16),
        compiler_params=_params("arbitrary", "arbitrary"),
    )(page_table, qm, wcol, iknew, *([pool] * pps))


def _class_reduce(x, n_cls, op):
    return _class_finish(_lane_fold(x, op), n_cls, op)


def _lane_fold(x, op):
    r = x[:, 0:LANES]
    for v in range(1, x.shape[1] // LANES):
        r = op(r, x[:, v * LANES:(v + 1) * LANES])
    return r


def _class_finish(r, n_cls, op):
    sh = LANES // 2
    while sh >= n_cls:
        r = op(r, pltpu.roll(r, sh, 1))
        sh //= 2
    return r


def _tile_lanes(r, ncol):
    return jnp.concatenate([r] * (ncol // LANES), axis=1)


def _class_column(r, n_cls):
    return jnp.concatenate([r[:, c:c + 1] for c in range(n_cls)], axis=0)


def _suffix_by_class(x, n_cls):
    lane = lax.broadcasted_iota(jnp.int32, (x.shape[0], LANES), 1)
    nv = x.shape[1] // LANES
    excl_in, tots = [], []
    for v in range(nv):
        xv = x[:, v * LANES:(v + 1) * LANES]
        inc = xv
        sh = n_cls
        while sh < LANES:
            inc = inc + jnp.where(lane < LANES - sh, pltpu.roll(inc, LANES - sh, 1), 0.0)
            sh *= 2
        tot = xv
        sh = LANES // 2
        while sh >= n_cls:
            tot = tot + pltpu.roll(tot, sh, 1)
            sh //= 2
        excl_in.append(inc - xv)
        tots.append(tot)
    later = jnp.zeros((x.shape[0], LANES), F32)
    out = [None] * nv
    for v in reversed(range(nv)):
        out[v] = excl_in[v] + later
        later = later + tots[v]
    return jnp.concatenate(out, axis=1), later


def _paged_kernel(pt_ref, *refs, kind, pages_per_step, n_cls, n_grp, scale, lambda_init):
    n_in = 3 + (2 if kind == "dsa" else 0) + 2 * pages_per_step + (2 if kind == "diff" else 0)
    ins, o_ref, (m_ref, l_ref, acc_ref) = refs[:n_in], refs[n_in], refs[n_in + 1:]
    q_ref, knew_ref, vnew_ref = ins[:3]
    pos = 3
    if kind == "dsa":
        selnew_ref, sel_ref = ins[pos:pos + 2]
        pos += 2
    k_refs = ins[pos:pos + pages_per_step]
    v_refs = ins[pos + pages_per_step:pos + 2 * pages_per_step]
    pos += 2 * pages_per_step
    if kind == "diff":
        lam_ref, g_ref = ins[pos:pos + 2]
    s_id = pl.program_id(1)
    page = knew_ref.shape[0]
    ncol = page * n_cls
    col = lax.broadcasted_iota(jnp.int32, (SUBLANES, ncol), 1)
    cls = col % n_cls

    def fold(s_big):
        out = s_big[0:SUBLANES, :]
        for c in range(1, n_cls):
            out = jnp.where(cls == c, s_big[c * SUBLANES:(c + 1) * SUBLANES, :], out)
        return out

    def unfold(p):
        return jnp.concatenate([jnp.where(cls == c, p, 0.0) for c in range(n_cls)], axis=0).astype(BF16)

    def widen(x):
        lane = lax.broadcasted_iota(jnp.int32, (SUBLANES, LANES), 1)
        per = LANES // n_cls
        return jnp.concatenate([jnp.take_along_axis(x, v * per + lane // n_cls, axis=1)
                                for v in range(ncol // LANES)], axis=1)

    def process(pages):
        v2s = [v_ref[...].reshape(ncol, v_ref.shape[-1]).astype(BF16) for _, v_ref, _, _ in pages]
        weights, alphas = [], []
        for g in range(n_grp):
            ss = []
            for k_ref, _, mask, bias in pages:
                k = k_ref[...] if n_grp == 1 else k_ref[:, pl.ds(g, n_cls, stride=n_grp), :]
                s = fold(_dot_t(q_ref[g], k.reshape(ncol, k.shape[-1]).astype(BF16)) * scale)
                if bias is not None:
                    s = s + bias
                if mask is not None and kind != "sb":
                    s = jnp.where(mask, s, NEG_BIG)
                ss.append(s)
            ws = []
            if kind == "sb":
                carry = m_ref[g]
                for (_, _, mask, _), s in zip(pages, ss):
                    ls, l1m = _log_sigmoid_pair(s)
                    if mask is not None:
                        l1m = jnp.where(mask, l1m, 0.0)
                    excl, total = _suffix_by_class(l1m, n_cls)
                    a = jnp.exp(ls + excl + _tile_lanes(carry, ncol))
                    ws.append(a if mask is None else jnp.where(mask, a, 0.0))
                    carry = carry + total
                m_ref[g] = carry
            else:
                m_old = m_ref[g]
                m_blk = None
                for s in ss:
                    r = _lane_fold(s, jnp.maximum)
                    m_blk = r if m_blk is None else jnp.maximum(m_blk, r)
                m_new = jnp.maximum(m_old, _class_finish(m_blk, n_cls, jnp.maximum))
                mt = _tile_lanes(m_new, ncol)
                l_blk = None
                for (_, _, mask, _), s in zip(pages, ss):
                    p = jnp.exp(s - mt)
                    if mask is not None:
                        p = jnp.where(mask, p, 0.0)
                    r = _lane_fold(p, jnp.add)
                    l_blk = r if l_blk is None else l_blk + r
                    ws.append(p)
                alpha = jnp.exp(m_old - m_new)
                l_ref[g] = alpha * l_ref[g] + _class_finish(l_blk, n_cls, jnp.add)
                m_ref[g] = m_new
                alphas.append(_class_column(alpha, n_cls))
            weights.append(ws)
        acc_blk = None
        for c, v2 in enumerate(v2s):
            big = [unfold(weights[g][c]) for g in range(n_grp)]
            part = _dot(big[0] if n_grp == 1 else jnp.concatenate(big, axis=0), v2)
            acc_blk = part if acc_blk is None else acc_blk + part
        if kind == "sb":
            acc_ref[...] += acc_blk
        else:
            alpha_col = alphas[0] if n_grp == 1 else jnp.concatenate(alphas, axis=0)
            acc_ref[...] = acc_ref[...] * alpha_col + acc_blk

    @pl.when(s_id == 0)
    def _():
        acc_ref[...] = jnp.zeros_like(acc_ref)
        l_ref[...] = jnp.zeros_like(l_ref)
        m_ref[...] = jnp.full_like(m_ref, {"sb": 0.0, "diff": NEG_BIG, "dsa": M_INIT}[kind])
        t = lax.broadcasted_iota(jnp.int32, (SUBLANES, ncol), 0)
        slot = col // n_cls
        if kind == "sb":
            process([(knew_ref, vnew_ref, slot < t, None)])
        elif kind == "diff":
            process([(knew_ref, vnew_ref, slot <= t, None)])
        else:
            process([(knew_ref, vnew_ref, None, widen(selnew_ref[...].astype(F32)))])

    pages = []
    for c in range(pages_per_step):
        bias = None
        if kind == "dsa":
            lo = (pages_per_step - 1 - c) * page
            bias = widen(sel_ref[:, lo:lo + page].astype(F32))
        pages.append((k_refs[c], v_refs[c], None, bias))
    process(pages)

    @pl.when(s_id == pl.num_programs(1) - 1)
    def _():
        if kind == "sb":
            o_ref[...] = acc_ref[...].astype(o_ref.dtype)
        elif kind == "dsa":
            o_ref[...] = (acc_ref[...] / _class_column(l_ref[0], n_cls)).astype(o_ref.dtype)
        else:
            lam = _lambda_full(lam_ref, lambda_init)
            rows = n_cls * SUBLANES
            a0 = acc_ref[0:rows, :] / _class_column(l_ref[0], n_cls)
            a1 = acc_ref[rows:2 * rows, :] / _class_column(l_ref[1], n_cls)
            o_ref[...] = _sub_norm(a0 - lam * a1, g_ref[...], lambda_init).astype(o_ref.dtype)


def _paged_attention(kind, page_table, layer, q, knew, vnew, kpool, vpool, n_pages, *,
                     sel=None, lam=None, g_sub=None, lambda_init=0.0):
    db, n_grp, rows, hd = q.shape
    n_cls = rows // SUBLANES
    page = kpool.shape[2]
    dv = vpool.shape[4]
    ncol = page * n_cls
    pps = 8 if n_pages % 8 == 0 else (4 if n_pages % 4 == 0 else 1)
    n_steps = n_pages // pps

    def pool_spec(pool, c):
        return pl.BlockSpec((None, None) + pool.shape[2:],
                            lambda b, s, pt: (layer, pt[b, n_pages - 1 - (s * pps + c)], 0, 0, 0))

    in_specs = [pl.BlockSpec((None,) + q.shape[1:], lambda b, s, pt: (b, 0, 0, 0)),
                pl.BlockSpec((None,) + knew.shape[1:], lambda b, s, pt: (b, 0, 0, 0)),
                pl.BlockSpec((None,) + vnew.shape[1:], lambda b, s, pt: (b, 0, 0, 0))]
    args = [q, knew, vnew]
    if kind == "dsa":
        in_specs += [pl.BlockSpec((None, SUBLANES, page), lambda b, s, pt: (b, 0, n_pages)),
                     pl.BlockSpec((None, SUBLANES, pps * page), lambda b, s, pt: (b, 0, n_steps - 1 - s))]
        args += [sel, sel]
    in_specs += [pool_spec(kpool, c) for c in range(pps)] + [pool_spec(vpool, c) for c in range(pps)]
    args += [kpool] * pps + [vpool] * pps
    if kind == "diff":
        in_specs += [pl.BlockSpec(lam.shape, lambda b, s, pt: (0, 0)),
                     pl.BlockSpec((1, dv), lambda b, s, pt: (0, 0))]
        args += [lam, g_sub.reshape(1, dv)]
    grid_spec = pltpu.PrefetchScalarGridSpec(
        num_scalar_prefetch=1,
        grid=(db, n_steps),
        in_specs=in_specs,
        out_specs=pl.BlockSpec((None, rows, dv), lambda b, s, pt: (b, 0, 0)),
        scratch_shapes=[pltpu.VMEM((n_grp, SUBLANES, LANES), F32), pltpu.VMEM((n_grp, SUBLANES, LANES), F32),
                        pltpu.VMEM((n_grp * rows, dv), F32)],
    )
    return pl.pallas_call(
        functools.partial(_paged_kernel, kind=kind, pages_per_step=pps, n_cls=n_cls, n_grp=n_grp,
                          scale=hd ** -0.5, lambda_init=lambda_init),
        grid_spec=grid_spec,
        out_shape=jax.ShapeDtypeStruct((db, rows, dv), BF16),
        compiler_params=_params("arbitrary", "arbitrary"),
    )(page_table, *args)


def _rope_tables(pos, half, ident_from=None):
    inv = ROPE_THETA ** (-jnp.arange(half, dtype=F32) / half)
    ang = pos.astype(F32)[:, None] * inv
    cos, sin = jnp.cos(ang), jnp.sin(ang)
    reps = LANES // (2 * half)
    c = jnp.tile(jnp.concatenate([cos, cos], axis=1), (1, reps))
    s = jnp.tile(jnp.concatenate([-sin, sin], axis=1), (1, reps))
    if ident_from is not None:
        lane = jnp.arange(LANES)[None, :]
        c = jnp.where(lane < ident_from, c, 1.0)
        s = jnp.where(lane < ident_from, s, 0.0)
    return c, s


def _expand_history(hist):
    nseq, _, c = hist.shape
    z = jnp.zeros((nseq, SUBLANES - 2, c), hist.dtype)
    blocks = jnp.concatenate([z, hist], axis=1).reshape(nseq * SUBLANES, c)
    return jnp.roll(blocks, -SUBLANES, axis=0)


def _last_two(st, nseq):
    return st.reshape(nseq, SUBLANES, st.shape[1])[:, SUBLANES - 2:, :]


def _pad_rows(x, rows):
    return jnp.pad(x, ((0, 0), (0, rows - x.shape[1])) + ((0, 0),) * (x.ndim - 2))


def _rows_by_class(x, db, ds, n_grp, n_cls):
    hd = x.shape[1] // (n_cls * n_grp)
    return x.reshape(db, ds, n_cls, n_grp, hd).transpose(0, 3, 2, 1, 4).reshape(db, n_grp, n_cls * ds, hd)


def _rows_by_query(y, db, ds):
    n_cls, dv = y.shape[1] // ds, y.shape[2]
    return y.reshape(db, n_cls, ds, dv).transpose(0, 2, 1, 3).reshape(db * ds, n_cls * dv)


def kernel(x_prompt, x_sample, state_conv_mix, cache_sb_k, cache_sb_v, cache_diff_k, cache_diff_v, cache_dsa_k, cache_dsa_v, cache_dsa_idx_k, state_ffn_conv, page_table, g_mix, g_ffn, g_final, w_a_in, conv_a, w_a_out, w_b_qkv, w_b_out, w_c_qkv, lam_c, g_c_subln, w_c_out, w_d_in, w_d_out, w_ffn_up, conv_ffn, w_ffn_down):
    batch, seq, d = x_prompt.shape
    db, ds, _ = x_sample.shape
    depth = g_mix.shape[0]
    n_pool, page = cache_sb_k.shape[1], cache_sb_k.shape[2]
    n_heads, hd = cache_sb_k.shape[3], cache_sb_k.shape[4]
    n_dh = cache_diff_k.shape[3]
    idim = cache_dsa_idx_k.shape[3]
    n_pages = page_table.shape[1]
    past = n_pages * page
    n_ih = (w_d_in.shape[2] - 3 * d - idim) // (idim + 1)
    assert ds == SUBLANES and hd == LANES and cache_diff_k.shape[5] == hd and n_dh * 2 == n_heads
    assert (n_ih * idim) % LANES == 0 and idim + n_ih <= LANES

    mp, ms = batch * seq, db * ds
    tm_p = min(512, seq)
    streams = (("p", tm_p, seq), ("s", ms, ds))
    x = {"p": x_prompt.reshape(mp, d), "s": x_sample.reshape(ms, d)}
    pos = {"p": jnp.arange(seq), "s": past + jnp.arange(ds)}
    pos_rows = {"p": pos["p"], "s": jnp.tile(pos["s"], db)}
    rope_blocks = {"p": seq // tm_p, "s": 1}
    rope_hd = {n: _rope_tables(pos_rows[n], hd // 2) for n in x}
    rope_idx = {n: _rope_tables(pos_rows[n], idim // 2) for n in x}
    rope_tail = {n: _rope_tables(pos_rows[n], idim // 2, ident_from=idim) for n in x}
    k_top = {"p": min(TOPK_MAX, seq // 4), "s": min(TOPK_MAX, (past + ds) // 4)}
    nseq = {"p": batch, "s": db}

    outs = {name: {"p": [], "s": []} for name in
            ("conv", "sbk", "sbv", "dk", "dv", "ak", "av", "ai", "fc")}

    def shaped(a, n, tail):
        return a.reshape((nseq[n], seq if n == "p" else ds) + tail)

    for i in range(depth):
        kind, j = i % 4, i // 4
        for n, tm, slen in streams:
            xb = _rmsnorm(x[n], g_mix[i], BF16, tm)
            if kind == 0:
                hist = None if n == "p" else _expand_history(state_conv_mix[j])
                y, st = _convmix(xb, w_a_in, conv_a, j, hist, tm, slen)
                outs["conv"][n].append(_last_two(st, nseq[n]))
                w_out = w_a_out
            elif kind == 1:
                w = w_b_qkv
                (qb,) = _proj(xb, w, j, 0, d, (BF16,), tm)
                kf, kb = _proj(xb, w, j, d, d, (F32, BF16), tm)
                vf, vb = _proj(xb, w, j, 2 * d, d, (F32, BF16), tm)
                outs["sbk"][n].append(shaped(kf, n, (n_heads, hd)))
                outs["sbv"][n].append(shaped(vf, n, (n_heads, hd)))
                if n == "p":
                    y = _sb_prompt(qb, kb, vb, batch, seq, n_heads, hd)
                else:
                    y = _rows_by_query(_paged_attention(
                        "sb", page_table, j, _rows_by_class(qb, db, ds, 1, n_heads),
                        _pad_rows(kf.reshape(db, ds, n_heads, hd), page),
                        _pad_rows(vf.reshape(db, ds, n_heads, hd), page),
                        cache_sb_k, cache_sb_v, n_pages), db, ds)
                w_out = w_b_out
            elif kind == 2:
                lambda_init = 0.8 - 0.6 * math.exp(-0.3 * i)
                w = w_c_qkv
                rope = rope_hd[n] + (hd // 2, rope_blocks[n])
                (qb,) = _proj(xb, w, j, 0, d, (BF16,), tm, rope)
                kf, kb = _proj(xb, w, j, d, d, (F32, BF16), tm, rope)
                vf, vb = _proj(xb, w, j, 2 * d, d, (F32, BF16), tm)
                outs["dk"][n].append(shaped(kf, n, (n_dh, 2, hd)))
                outs["dv"][n].append(shaped(vf, n, (n_dh, 2 * hd)))
                if n == "p":
                    y = _diff_prompt(qb, kb, vb, lam_c[j], g_c_subln[j], batch, seq, n_dh, hd, lambda_init)
                else:
                    y = _rows_by_query(_paged_attention(
                        "diff", page_table, j, _rows_by_class(qb, db, ds, 2, n_dh),
                        _pad_rows(kf.reshape(db, ds, n_heads, hd), page),
                        _pad_rows(vf.reshape(db, ds, n_dh, 2 * hd), page),
                        cache_diff_k.reshape(cache_diff_k.shape[:3] + (n_heads, hd)), cache_diff_v, n_pages,
                        lam=lam_c[j], g_sub=g_c_subln[j], lambda_init=lambda_init), db, ds)
                w_out = w_c_out
            else:
                w = w_d_in
                rope = rope_hd[n] + (hd // 2, rope_blocks[n])
                (qb,) = _proj(xb, w, j, 0, d, (BF16,), tm, rope)
                kf, kb = _proj(xb, w, j, d, d, (F32, BF16), tm, rope)
                vf, vb = _proj(xb, w, j, 2 * d, d, (F32, BF16), tm)
                (iqb,) = _proj(xb, w, j, 3 * d, n_ih * idim, (BF16,), tm, rope_idx[n] + (idim // 2, rope_blocks[n]))
                o2 = 3 * d + n_ih * idim
                w_ik, w_iw = w[j:j + 1, :, o2:o2 + idim], w[j:j + 1, :, o2 + idim:]
                w_tail = jnp.concatenate([w_ik, w_iw, jnp.zeros((1, d, LANES - idim - n_ih), F32)], axis=2)
                (tail,) = _proj(xb, w_tail, 0, 0, LANES, (F32,), tm, rope_tail[n] + (idim // 2, rope_blocks[n]))
                outs["ak"][n].append(shaped(kf, n, (n_heads, hd)))
                outs["av"][n].append(shaped(vf, n, (n_heads, hd)))
                outs["ai"][n].append(shaped(tail[:, :idim], n, (idim,)))
                if n == "p":
                    w_ik2 = jnp.concatenate([w_ik] * (LANES // idim), axis=2)
                    (ik2b,) = _proj(xb, w_ik2, 0, 0, LANES, (BF16,), tm, rope_idx[n] + (idim // 2, rope_blocks[n]))
                    sel = _select_prompt(iqb, ik2b, tail, batch, seq, n_ih, idim, k_top[n])
                    y = _dsa_prompt(qb, kb, vb, sel, batch, seq, n_heads, hd)
                else:
                    qm = iqb.reshape(db, ds, n_ih, idim).transpose(0, 2, 1, 3).reshape(db, n_ih * ds, idim)
                    wcol = tail[:, idim:idim + n_ih].reshape(db, ds, n_ih).transpose(0, 2, 1).reshape(db, n_ih * ds, 1)
                    wcol = wcol * ((idim ** -0.5) * (n_ih ** -0.5))
                    iknew = _pad_rows(tail[:, :idim].reshape(db, ds, idim), page)
                    sel = _select_sample(page_table, qm, wcol, iknew, cache_dsa_idx_k, j, n_pages, k_top[n])
                    y = _rows_by_query(_paged_attention(
                        "dsa", page_table, j, _rows_by_class(qb, db, ds, 1, n_heads),
                        _pad_rows(kf.reshape(db, ds, n_heads, hd), page),
                        _pad_rows(vf.reshape(db, ds, n_heads, hd), page),
                        cache_dsa_k, cache_dsa_v, n_pages, sel=sel), db, ds)
                w_out = w_d_out
            x[n] = _mm_res(y, w_out, j, x[n], tm)
            fb = _rmsnorm(x[n], g_ffn[i], BF16, tm)
            hist = None if n == "p" else _expand_history(state_ffn_conv[i])
            tm_up = 2 * ROW_BLOCK if (n == "p" and slen % (2 * ROW_BLOCK) == 0) else tm
            act, st = _ffn_up(fb, w_ffn_up, conv_ffn, i, hist, tm_up, slen)
            outs["fc"][n].append(_last_two(st, nseq[n]))
            x[n] = _mm_res(act, w_ffn_down, i, x[n], tm)

    y_prompt = _rmsnorm(x["p"], g_final, F32, tm_p).reshape(batch, seq, d)
    y_sample = _rmsnorm(x["s"], g_final, F32, ms).reshape(db, ds, d)
    res = [y_prompt, y_sample]
    for name in ("conv", "sbk", "sbv", "dk", "dv", "ak", "av", "ai", "fc"):
        for n in ("p", "s"):
            res.append(jnp.stack(outs[name][n]))
    return tuple(res)
```
